```python
import math
import jax, jax.numpy as jnp
from jax import lax
import numpy as np

D_MODEL = 1024
BATCH = 2
SEQ = 8192
DEPTH = 2

CHUNK = 64
EPS = 1e-6
N_BRANCHES = 2
D_CONV = D_MODEL
CONV_KERNEL = 31
D_POOL = D_MODEL
POOL_WINDOWS = (2, 4, 8, 16)
N_POOL_GROUPS = len(POOL_WINDOWS)
POOL_GROUP = D_POOL // N_POOL_GROUPS
D_IN = 2 * D_CONV + D_POOL + N_BRANCHES * D_MODEL
N_MOD = 6
PEER_HEADS = 8
PEER_NKEYS = 128
PEER_EXPERTS = PEER_NKEYS * PEER_NKEYS
PEER_DKEY = 256
PEER_DHALF = PEER_DKEY // 2
PEER_TOPK = 16
PEER_BLOCK = 128

kernel_name = "hybrid_gated_conv_pool_peer_encoder"


def _rmsnorm(x, g):
    x32 = x.astype(jnp.float32)
    y = x32 * lax.rsqrt(jnp.mean(x32 * x32, axis=-1, keepdims=True) + EPS)
    return (y * g.astype(jnp.float32)).astype(x.dtype)


def _modulate(h, shift, scale):
    return h * (1 + scale[:, None, :]) + shift[:, None, :]


def _conv_branch(val, gate, conv_w, conv_b, ln_g, ln_b, w_co):
    u = val * jax.nn.sigmoid(gate)
    u = lax.conv_general_dilated(
        u, conv_w[:, None, :], window_strides=(1,),
        padding=[(CONV_KERNEL - 1, 0)],
        dimension_numbers=("NWC", "WIO", "NWC"),
        feature_group_count=D_CONV) + conv_b
    u32 = u.astype(jnp.float32)
    mu = jnp.mean(u32, axis=-1, keepdims=True)
    var = jnp.mean(jnp.square(u32 - mu), axis=-1, keepdims=True)
    u = ((u32 - mu) * lax.rsqrt(var + EPS) * ln_g.astype(jnp.float32)
         + ln_b.astype(jnp.float32)).astype(val.dtype)
    return jax.nn.silu(u) @ w_co


def _causal_pool(p):
    B, S, _ = p.shape
    p32 = p.astype(jnp.float32)
    csp = jnp.concatenate([jnp.zeros((B, 1, D_POOL), jnp.float32), jnp.cumsum(p32, axis=1)], axis=1)
    t = jnp.arange(S)
    outs = []
    for g, w in enumerate(POOL_WINDOWS):
        lo, hi = g * POOL_GROUP, (g + 1) * POOL_GROUP
        sl = csp[..., lo:hi]
        upper = sl[:, 1:]
        lower = jnp.concatenate([jnp.zeros((B, w - 1, POOL_GROUP), jnp.float32), sl[:, :S + 1 - w]], axis=1)
        cnt = jnp.minimum(t + 1, w).astype(jnp.float32)
        outs.append((upper - lower) / cnt[None, :, None] - p32[..., lo:hi])
    return jnp.stack(outs, axis=2).astype(p.dtype)


def _peer(h, wq, keys, u_tab, v_tab):
    B, S, D = h.shape
    tokens = h.reshape(-1, PEER_BLOCK, D)

    def block(ht):
        T = ht.shape[0]
        q = (ht @ wq).reshape(T, PEER_HEADS, 2, PEER_DHALF)
        s = jnp.einsum("thpk,hpnk->thpn", q, keys).astype(jnp.float32)
        sv, si = lax.top_k(s, PEER_TOPK)
        cand = sv[:, :, 0, :, None] + sv[:, :, 1, None, :]
        cv, ci = lax.top_k(cand.reshape(T, PEER_HEADS, PEER_TOPK * PEER_TOPK), PEER_TOPK)
        i1 = jnp.take_along_axis(si[:, :, 0], ci // PEER_TOPK, axis=-1)
        i2 = jnp.take_along_axis(si[:, :, 1], ci % PEER_TOPK, axis=-1)
        expert = i1 * PEER_NKEYS + i2
        gw = jax.nn.softmax(cv, axis=-1).astype(ht.dtype)
        u_sel = u_tab[expert]
        v_sel = v_tab[expert]
        act = jax.nn.gelu(jnp.einsum("thkd,td->thk", u_sel, ht), approximate=False)
        return jnp.einsum("thk,thkd->td", gw * act, v_sel)

    return lax.map(block, tokens).reshape(B, S, D)


def setup_inputs(seed: int = 0) -> dict:
    key = jax.random.key(seed)
    ks = jax.random.split(key, 24)
    f32 = jnp.float32
    L, D = DEPTH, D_MODEL
    nrm = lambda k, shape, s: jax.random.normal(k, shape, f32) * s
    return {
        "x": nrm(ks[0], (BATCH, SEQ, D), 1.0),
        "c": nrm(ks[1], (BATCH, D), 1.0),
        "ada_w": nrm(ks[2], (L, D, N_MOD * D), 0.5 * D ** -0.5),
        "ada_b": nrm(ks[3], (L, N_MOD * D), 0.02),
        "norm1_g": 1.0 + nrm(ks[4], (L, D), 0.05),
        "w_in": nrm(ks[5], (L, D, D_IN), D ** -0.5),
        "b_in": nrm(ks[6], (L, D_IN), 0.02),
        "conv_w": nrm(ks[7], (L, CONV_KERNEL, D_CONV), CONV_KERNEL ** -0.5),
        "conv_b": nrm(ks[8], (L, D_CONV), 0.02),
        "conv_ln_g": 1.0 + nrm(ks[9], (L, D_CONV), 0.05),
        "conv_ln_b": nrm(ks[10], (L, D_CONV), 0.02),
        "w_conv_out": nrm(ks[11], (L, D_CONV, D), D_CONV ** -0.5),
        "pool_w": nrm(ks[12], (L, N_POOL_GROUPS, POOL_GROUP, POOL_GROUP), POOL_GROUP ** -0.5),
        "pool_scale": 0.5 + nrm(ks[13], (L, D_POOL), 0.1),
        "w_out": nrm(ks[14], (L, D, D), D ** -0.5),
        "b_out": nrm(ks[15], (L, D), 0.02),
        "norm2_g": 1.0 + nrm(ks[16], (L, D), 0.05),
        "peer_wq": nrm(ks[17], (L, D, PEER_HEADS * PEER_DKEY), D ** -0.5),
        "peer_keys": nrm(ks[18], (L, PEER_HEADS, 2, PEER_NKEYS, PEER_DHALF), PEER_DHALF ** -0.5),
        "peer_u": nrm(ks[19], (L, PEER_EXPERTS, D), D ** -0.5),
        "peer_v": nrm(ks[20], (L, PEER_EXPERTS, D), 1.0),
        "final_g": 1.0 + nrm(ks[21], (D,), 0.05),
    }


def reference(x, c, ada_w, ada_b, norm1_g, w_in, b_in, conv_w, conv_b, conv_ln_g, conv_ln_b,
              w_conv_out, pool_w, pool_scale, w_out, b_out, norm2_g, peer_wq, peer_keys,
              peer_u, peer_v, final_g):
    B, S, D = x.shape
    c_act = jax.nn.silu(c)
    for l in range(DEPTH):
        mod = c_act @ ada_w[l] + ada_b[l]
        sh1, sc1, g1, sh2, sc2, g2 = jnp.split(mod, N_MOD, axis=-1)

        h = _modulate(_rmsnorm(x, norm1_g[l]), sh1, sc1)
        z = h @ w_in[l] + b_in[l]
        za, zb, zp, zg = jnp.split(z, [D_CONV, 2 * D_CONV, 2 * D_CONV + D_POOL], axis=-1)
        conv_out = _conv_branch(za, zb, conv_w[l], conv_b[l], conv_ln_g[l], conv_ln_b[l], w_conv_out[l])
        pooled = _causal_pool(zp)
        pool_out = jnp.einsum("bsgc,gcd->bsgd", pooled, pool_w[l]).reshape(B, S, D_POOL) * pool_scale[l]
        gates = jax.nn.sigmoid(zg).reshape(B, S, N_BRANCHES, D)
        y = gates[:, :, 0] * conv_out + gates[:, :, 1] * pool_out
        x = x + g1[:, None, :] * (y @ w_out[l] + b_out[l])

        h = _modulate(_rmsnorm(x, norm2_g[l]), sh2, sc2)
        x = x + g2[:, None, :] * _peer(h, peer_wq[l], peer_keys[l], peer_u[l], peer_v[l])
    return _rmsnorm(x, final_g)
```

```python
import functools
import math

import jax
import jax.numpy as jnp
from jax import lax
from jax.experimental import pallas as pl
from jax.experimental.pallas import tpu as pltpu

F32 = jnp.float32
BF16 = jnp.bfloat16

EPS = 1e-6
CONV_KERNEL = 31
POOL_WINDOWS = (2, 4, 8, 16)
N_MOD = 6
PEER_HEADS = 8
PEER_NKEYS = 128
PEER_TOPK = 16

LANES = 128
SUBLANES = 8
VMEM_LIMIT_BYTES = 56 * 1024 * 1024

MIXER_ROWS = 512
CONV_HALO = 32
POOL_HALO = 16
CONV_CHUNK = 8
ROUTE_COLS = 512
PEER_COLS = 512
PEER_EXPERT_TILE = 1024

NEG_INF = float("-inf")


def _rmsnorm_mod(x, g, shift, scale):
    y = x * lax.rsqrt(jnp.mean(x * x, axis=-1, keepdims=True) + EPS) * g
    return y * (1.0 + scale) + shift


def _ada_kernel(c_ref, w_ref, b_ref, o_ref):
    c = c_ref[...]
    act = c * jax.nn.sigmoid(c)
    o_ref[0] = jnp.dot(act, w_ref[0], preferred_element_type=F32,
                       precision=lax.Precision.HIGHEST) + b_ref[0]


def _ada_call(c_pad, ada_w, ada_b):
    L, D, ND = ada_w.shape
    rows = c_pad.shape[0]
    return pl.pallas_call(
        _ada_kernel,
        grid=(L, ND // D),
        in_specs=[
            pl.BlockSpec((rows, D), lambda l, n: (0, 0)),
            pl.BlockSpec((1, D, D), lambda l, n: (l, 0, n)),
            pl.BlockSpec((1, 1, D), lambda l, n: (l, 0, n)),
        ],
        out_specs=pl.BlockSpec((1, rows, D), lambda l, n: (l, 0, n)),
        out_shape=jax.ShapeDtypeStruct((L, rows, ND), F32),
        compiler_params=pltpu.CompilerParams(
            dimension_semantics=("arbitrary", "arbitrary"), vmem_limit_bytes=VMEM_LIMIT_BYTES),
        name="ada_mod",
    )(c_pad, ada_w, ada_b.reshape(L, 1, ND))


def _mixer_kernel(x_ref, mod_ref, n1g_ref, win_ref, bin_ref, cw_ref, cb_ref, lng_ref, lnb_ref,
                  wco_ref, pw_ref, ps_ref, wout_ref, bout_ref, o_ref, ubuf, pbuf, cbuf):
    ts, D = cbuf.shape
    s = pl.program_id(1)

    @pl.when(s == 0)
    def _():
        ubuf[0:CONV_HALO, :] = jnp.zeros((CONV_HALO, D), F32)
        pbuf[0:POOL_HALO, :] = jnp.zeros((POOL_HALO, D), F32)

    @pl.when(s > 0)
    def _():
        ubuf[0:CONV_HALO, :] = ubuf[ts:ts + CONV_HALO, :]
        pbuf[0:POOL_HALO, :] = pbuf[ts:ts + POOL_HALO, :]

    x = x_ref[0]
    shift1 = mod_ref[0, 0:1, :]
    scale1 = mod_ref[0, 1:2, :]
    gate1 = mod_ref[0, 2:3, :]
    h = _rmsnorm_mod(x, n1g_ref[...], shift1, scale1).astype(BF16)

    def proj(j):
        cols = slice(j * D, (j + 1) * D)
        return jnp.dot(h, win_ref[:, cols], preferred_element_type=F32) + bin_ref[:, cols]

    za = proj(0)
    zb = proj(1)
    ubuf[CONV_HALO:CONV_HALO + ts, :] = za * jax.nn.sigmoid(zb)
    zp = proj(2)
    pbuf[POOL_HALO:POOL_HALO + ts, :] = zp

    lead = CONV_HALO - (CONV_KERNEL - 1)

    def conv_chunk(i, carry):
        r0 = pl.multiple_of(i * CONV_CHUNK, CONV_CHUNK)
        win = ubuf[pl.ds(r0, CONV_CHUNK + CONV_HALO), :]
        acc = jnp.broadcast_to(cb_ref[...], (CONV_CHUNK, D))
        for k in range(CONV_KERNEL):
            acc = acc + cw_ref[k:k + 1, :] * win[lead + k:lead + k + CONV_CHUNK, :]
        cbuf[pl.ds(r0, CONV_CHUNK), :] = acc
        return carry

    lax.fori_loop(0, ts // CONV_CHUNK, conv_chunk, 0)

    u = cbuf[...]
    mu = jnp.mean(u, axis=-1, keepdims=True)
    uc = u - mu
    var = jnp.mean(uc * uc, axis=-1, keepdims=True)
    un = uc * lax.rsqrt(var + EPS) * lng_ref[...] + lnb_ref[...]
    act = (un * jax.nn.sigmoid(un)).astype(BF16)
    conv_out = jnp.dot(act, wco_ref[...], preferred_element_type=F32)

    n_groups = len(POOL_WINDOWS)
    pg = D // n_groups
    frame = lax.broadcasted_iota(jnp.int32, (ts, pg), 0) + (s * ts + 1)
    pool_parts = []
    for g, w in enumerate(POOL_WINDOWS):
        cols = slice(g * pg, (g + 1) * pg)
        tot = pbuf[POOL_HALO:POOL_HALO + ts, cols]
        for d in range(1, w):
            tot = tot + pbuf[POOL_HALO - d:POOL_HALO - d + ts, cols]
        cnt = jnp.minimum(frame, w).astype(F32)
        pooled = tot / cnt - zp[:, cols]
        pool_parts.append(jnp.dot(pooled.astype(BF16), pw_ref[g], preferred_element_type=F32))
    pool_out = jnp.concatenate(pool_parts, axis=-1) * ps_ref[...]

    ga = jax.nn.sigmoid(proj(3))
    gb = jax.nn.sigmoid(proj(4))
    y = (ga * conv_out + gb * pool_out).astype(BF16)
    o_ref[0] = x + gate1 * (jnp.dot(y, wout_ref[...], preferred_element_type=F32) + bout_ref[...])


def _mixer_call(x, mod_l, n1g, w_in, b_in, conv_w, conv_b, ln_g, ln_b, w_co, pool_w, pool_scale, w_out, b_out):
    B, S, D = x.shape
    ts = MIXER_ROWS
    d_in = w_in.shape[1]
    n_groups, pg, _ = pool_w.shape
    const2 = lambda b, s: (0, 0)
    const3 = lambda b, s: (0, 0, 0)
    row = lambda v: v.reshape(1, -1)
    return pl.pallas_call(
        _mixer_kernel,
        grid=(B, S // ts),
        in_specs=[
            pl.BlockSpec((1, ts, D), lambda b, s: (b, s, 0)),
            pl.BlockSpec((1, N_MOD, D), lambda b, s: (b, 0, 0)),
            pl.BlockSpec((1, D), const2),
            pl.BlockSpec((D, d_in), const2),
            pl.BlockSpec((1, d_in), const2),
            pl.BlockSpec((CONV_KERNEL, D), const2),
            pl.BlockSpec((1, D), const2),
            pl.BlockSpec((1, D), const2),
            pl.BlockSpec((1, D), const2),
            pl.BlockSpec((D, D), const2),
            pl.BlockSpec((n_groups, pg, pg), const3),
            pl.BlockSpec((1, D), const2),
            pl.BlockSpec((D, D), const2),
            pl.BlockSpec((1, D), const2),
        ],
        out_specs=pl.BlockSpec((1, ts, D), lambda b, s: (b, s, 0)),
        out_shape=jax.ShapeDtypeStruct((B, S, D), F32),
        scratch_shapes=[
            pltpu.VMEM((CONV_HALO + ts, D), F32),
            pltpu.VMEM((POOL_HALO + ts, D), F32),
            pltpu.VMEM((ts, D), F32),
        ],
        compiler_params=pltpu.CompilerParams(
            dimension_semantics=("arbitrary", "arbitrary"), vmem_limit_bytes=VMEM_LIMIT_BYTES),
        name="mixer",
    )(x, mod_l, row(n1g), w_in.astype(BF16), row(b_in), conv_w, row(conv_b), row(ln_g), row(ln_b),
      w_co.astype(BF16), pool_w.astype(BF16), row(pool_scale), w_out.astype(BF16), row(b_out))


def _top_ranks(s):
    rank = jnp.full(s.shape, float(PEER_TOPK), F32)
    cur = s
    tops = []
    for r in range(PEER_TOPK):
        m = jnp.max(cur, axis=0, keepdims=True)
        hit = cur == m
        rank = jnp.where(hit, float(r), rank)
        cur = jnp.where(hit, NEG_INF, cur)
        tops.append(m)
    return rank, tops


def _route_kernel(x_ref, mod_ref, n2g_ref, wqT_ref, keys_ref, hT_ref, e1_ref, cnt_ref, r2_ref, e2_ref,
                  qT_scr, s_scr):
    tc = x_ref.shape[0]
    shift2 = mod_ref[0, 3:4, :]
    scale2 = mod_ref[0, 4:5, :]
    h = _rmsnorm_mod(x_ref[...], n2g_ref[...], shift2, scale2)
    hT = h.T.astype(BF16)
    hT_ref[...] = hT
    qT_scr[...] = jnp.dot(wqT_ref[...], hT, preferred_element_type=F32)

    row16 = lax.broadcasted_iota(jnp.int32, (PEER_TOPK, LANES), 0)
    row8 = lax.broadcasted_iota(jnp.int32, (SUBLANES, LANES), 0)

    def head_body(hd, carry):
        for p in range(2):
            k0 = pl.multiple_of(hd * (2 * PEER_NKEYS) + p * PEER_NKEYS, PEER_NKEYS)
            q_hp = qT_scr[pl.ds(k0, PEER_NKEYS), :]
            s_scr[p] = jnp.dot(keys_ref[hd, p], q_hp, preferred_element_type=F32,
                               precision=lax.Precision.HIGHEST)

        def chunk_body(c, carry2):
            l0 = pl.multiple_of(c * LANES, LANES)
            s1 = s_scr[0, :, pl.ds(l0, LANES)]
            s2 = s_scr[1, :, pl.ds(l0, LANES)]
            rank1, top1 = _top_ranks(s1)
            rank2, top2 = _top_ranks(s2)
            top2a = jnp.zeros((PEER_TOPK, LANES), F32)
            for r in range(PEER_TOPK):
                top2a = jnp.where(row16 == r, top2[r], top2a)

            slabs = [top1[0] + top2a]
            for i in range(1, PEER_TOPK):
                n_valid = PEER_TOPK // (i + 1)
                slab = top1[i] + top2a[0:SUBLANES]
                slabs.append(jnp.where(row8 < n_valid, slab, NEG_INF))
            cur = list(slabs)
            thr = None
            for r in range(PEER_TOPK):
                m8 = jnp.maximum(cur[0][0:SUBLANES], cur[0][SUBLANES:])
                for sl in cur[1:]:
                    m8 = jnp.maximum(m8, sl)
                thr = jnp.max(m8, axis=0, keepdims=True)
                if r + 1 < PEER_TOPK:
                    cur = [jnp.where(sl == thr, NEG_INF, sl) for sl in cur]

            cmax = top1[0] + top2[0]
            zsum = jnp.zeros((1, LANES), F32)
            cnt = jnp.zeros(s1.shape, F32)
            for i, slab in enumerate(slabs):
                sel = slab >= thr
                n_i = jnp.sum(jnp.where(sel, 1.0, 0.0), axis=0, keepdims=True)
                zsum = zsum + jnp.sum(jnp.where(sel, jnp.exp(slab - cmax), 0.0), axis=0, keepdims=True)
                cnt = jnp.where(rank1 == float(i), n_i, cnt)

            e1_ref[hd, :, pl.ds(l0, LANES)] = jnp.exp(s1 - top1[0])
            cnt_ref[hd, :, pl.ds(l0, LANES)] = cnt
            r2_ref[hd, :, pl.ds(l0, LANES)] = rank2
            e2_ref[hd, :, pl.ds(l0, LANES)] = jnp.exp(s2 - top2[0]) / zsum
            return carry2

        lax.fori_loop(0, tc // LANES, chunk_body, 0)
        return carry

    lax.fori_loop(0, PEER_HEADS, head_body, 0)


def _route_call(x2d, mod_l, n2g, wqT, keys, seq_len):
    T, D = x2d.shape
    tc = ROUTE_COLS
    HK = wqT.shape[0]
    steps_per_seq = seq_len // tc
    route_shape = jax.ShapeDtypeStruct((PEER_HEADS, PEER_NKEYS, T), F32)
    route_spec = pl.BlockSpec((PEER_HEADS, PEER_NKEYS, tc), lambda i: (0, 0, i))
    return pl.pallas_call(
        _route_kernel,
        grid=(T // tc,),
        in_specs=[
            pl.BlockSpec((tc, D), lambda i: (i, 0)),
            pl.BlockSpec((1, N_MOD, D), lambda i: (i // steps_per_seq, 0, 0)),
            pl.BlockSpec((1, D), lambda i: (0, 0)),
            pl.BlockSpec((HK, D), lambda i: (0, 0)),
            pl.BlockSpec(keys.shape, lambda i: (0, 0, 0, 0)),
        ],
        out_specs=[pl.BlockSpec((D, tc), lambda i: (0, i)), route_spec, route_spec, route_spec, route_spec],
        out_shape=[jax.ShapeDtypeStruct((D, T), BF16), route_shape, route_shape, route_shape, route_shape],
        scratch_shapes=[
            pltpu.VMEM((HK, tc), F32),
            pltpu.VMEM((2, PEER_NKEYS, tc), F32),
        ],
        compiler_params=pltpu.CompilerParams(
            dimension_semantics=("arbitrary",), vmem_limit_bytes=VMEM_LIMIT_BYTES),
        name="peer_route",
    )(x2d, mod_l, n2g.reshape(1, D), wqT, keys)


def _gelu_exact(x):
    return 0.5 * x * (1.0 + lax.erf(x * (1.0 / math.sqrt(2.0))))


def _peer_kernel(hT_ref, e1_ref, cnt_ref, r2_ref, e2_ref, u_ref, vT_ref, x_ref, mod_ref, fg_ref, o_ref,
                 acc, p_scr, *, final_norm):
    te = u_ref.shape[0]
    j = pl.program_id(1)

    @pl.when(j == 0)
    def _():
        acc[...] = jnp.zeros(acc.shape, F32)

    hT = hT_ref[...]
    for al in range(te // PEER_NKEYS):
        a = j * (te // PEER_NKEYS) + al
        rows = slice(al * PEER_NKEYS, (al + 1) * PEER_NKEYS)
        pre = jnp.dot(u_ref[rows, :], hT, preferred_element_type=F32)
        w = jnp.zeros(pre.shape, F32)
        for hd in range(PEER_HEADS):
            cnt_row = cnt_ref[hd, pl.ds(a, 1), :]
            e1_row = e1_ref[hd, pl.ds(a, 1), :]
            w = w + jnp.where(r2_ref[hd] < cnt_row, e2_ref[hd], 0.0) * e1_row
        p_scr[rows, :] = (w * _gelu_exact(pre)).astype(BF16)
    acc[...] += jnp.dot(vT_ref[...], p_scr[...], preferred_element_type=F32)

    @pl.when(j == pl.num_programs(1) - 1)
    def _():
        gate2 = mod_ref[0, 5:6, :]
        out = x_ref[...] + gate2 * acc[...].T
        if final_norm:
            out = out * lax.rsqrt(jnp.mean(out * out, axis=-1, keepdims=True) + EPS) * fg_ref[...]
        o_ref[...] = out


def _peer_call(hT, e1, cnt, r2, e2, u_bf, vT_bf, x2d, mod_l, final_g, seq_len, final_norm):
    T, D = x2d.shape
    E = u_bf.shape[0]
    tc = PEER_COLS
    te = PEER_EXPERT_TILE
    steps_per_seq = seq_len // tc
    route_spec = pl.BlockSpec((PEER_HEADS, PEER_NKEYS, tc), lambda i, j: (0, 0, i))
    return pl.pallas_call(
        functools.partial(_peer_kernel, final_norm=final_norm),
        grid=(T // tc, E // te),
        in_specs=[
            pl.BlockSpec((D, tc), lambda i, j: (0, i)),
            route_spec, route_spec, route_spec, route_spec,
            pl.BlockSpec((te, D), lambda i, j: (j, 0)),
            pl.BlockSpec((D, te), lambda i, j: (0, j)),
            pl.BlockSpec((tc, D), lambda i, j: (i, 0)),
            pl.BlockSpec((1, N_MOD, D), lambda i, j: (i // steps_per_seq, 0, 0)),
            pl.BlockSpec((1, D), lambda i, j: (0, 0)),
        ],
        out_specs=pl.BlockSpec((tc, D), lambda i, j: (i, 0)),
        out_shape=jax.ShapeDtypeStruct((T, D), F32),
        scratch_shapes=[
            pltpu.VMEM((D, tc), F32),
            pltpu.VMEM((te, tc), BF16),
        ],
        compiler_params=pltpu.CompilerParams(
            dimension_semantics=("arbitrary", "arbitrary"), vmem_limit_bytes=VMEM_LIMIT_BYTES),
        name="peer_experts",
    )(hT, e1, cnt, r2, e2, u_bf, vT_bf, x2d, mod_l, final_g.reshape(1, D))


@jax.jit
def _forward(x, c, ada_w, ada_b, norm1_g, w_in, b_in, conv_w, conv_b, conv_ln_g, conv_ln_b, w_conv_out,
             pool_w, pool_scale, w_out, b_out, norm2_g, peer_wq, peer_keys, peer_u, peer_v, final_g):
    B, S, D = x.shape
    L = ada_w.shape[0]
    c_pad = jnp.zeros((SUBLANES, D), F32).at[:B].set(c)
    mod = _ada_call(c_pad, ada_w, ada_b)[:, :B].reshape(L, B, N_MOD, D)
    for l in range(L):
        x = _mixer_call(x, mod[l], norm1_g[l], w_in[l], b_in[l], conv_w[l], conv_b[l], conv_ln_g[l],
                        conv_ln_b[l], w_conv_out[l], pool_w[l], pool_scale[l], w_out[l], b_out[l])
        x2d = x.reshape(B * S, D)
        hT, e1, cnt, r2, e2 = _route_call(x2d, mod[l], norm2_g[l], peer_wq[l].T.astype(BF16), peer_keys[l], S)
        x2d = _peer_call(hT, e1, cnt, r2, e2, peer_u[l].astype(BF16), peer_v[l].T.astype(BF16), x2d, mod[l],
                         final_g, S, final_norm=(l == L - 1))
        x = x2d.reshape(B, S, D)
    return x


def kernel(x, c, ada_w, ada_b, norm1_g, w_in, b_in, conv_w, conv_b, conv_ln_g, conv_ln_b, w_conv_out, pool_w,
           pool_scale, w_out, b_out, norm2_g, peer_wq, peer_keys, peer_u, peer_v, final_g):
    return _forward(x, c, ada_w, ada_b, norm1_g, w_in, b_in, conv_w, conv_b, conv_ln_g, conv_ln_b, w_conv_out,
                    pool_w, pool_scale, w_out, b_out, norm2_g, peer_wq, peer_keys, peer_u, peer_v, final_g)
```

```python
import functools
import math

import jax
import jax.numpy as jnp
from jax import lax
from jax.experimental import pallas as pl
from jax.experimental.pallas import tpu as pltpu

F32 = jnp.float32
BF16 = jnp.bfloat16

EPS = 1e-6
CONV_KERNEL = 31
POOL_WINDOWS = (2, 4, 8, 16)
N_MOD = 6
PEER_HEADS = 8
PEER_NKEYS = 128
PEER_TOPK = 16

LANES = 128
SUBLANES = 8
BF16_ROWS = 16
VMEM_LIMIT_BYTES = 56 * 1024 * 1024

MIXER_ROWS = 512
CONV_HALO = 32
POOL_HALO = 16
CONV_CHUNK = 8
ROUTE_COLS = 512
PEER_COLS = 512
PEER_EXPERT_TILE = 1024
PEER_PIECE = 256

NEG_INF = float("-inf")


def _rmsnorm_mod(x, g, shift, scale):
    y = x * lax.rsqrt(jnp.mean(x * x, axis=-1, keepdims=True) + EPS) * g
    return y * (1.0 + scale) + shift


def _ada_kernel(c_ref, w_ref, b_ref, o_ref):
    c = c_ref[...]
    act = c * jax.nn.sigmoid(c)
    o_ref[0] = jnp.dot(act, w_ref[0], preferred_element_type=F32,
                       precision=lax.Precision.HIGHEST) + b_ref[0]


def _ada_call(c_pad, ada_w, ada_b):
    L, D, ND = ada_w.shape
    rows = c_pad.shape[0]
    return pl.pallas_call(
        _ada_kernel,
        grid=(L, ND // D),
        in_specs=[
            pl.BlockSpec((rows, D), lambda l, n: (0, 0)),
            pl.BlockSpec((1, D, D), lambda l, n: (l, 0, n)),
            pl.BlockSpec((1, 1, D), lambda l, n: (l, 0, n)),
        ],
        out_specs=pl.BlockSpec((1, rows, D), lambda l, n: (l, 0, n)),
        out_shape=jax.ShapeDtypeStruct((L, rows, ND), F32),
        compiler_params=pltpu.CompilerParams(
            dimension_semantics=("arbitrary", "arbitrary"), vmem_limit_bytes=VMEM_LIMIT_BYTES),
        name="ada_mod",
    )(c_pad, ada_w, ada_b.reshape(L, 1, ND))


def _mixer_kernel(x_ref, mod_ref, n1g_ref, win_ref, bin_ref, cw_ref, cb_ref, lng_ref, lnb_ref,
                  wco_ref, pw_ref, ps_ref, wout_ref, bout_ref, o_ref, ubuf, pbuf, cbuf):
    ts, D = cbuf.shape
    s = pl.program_id(1)

    @pl.when(s == 0)
    def _():
        ubuf[0:CONV_HALO, :] = jnp.zeros((CONV_HALO, D), F32)
        pbuf[0:POOL_HALO, :] = jnp.zeros((POOL_HALO, D), F32)

    @pl.when(s > 0)
    def _():
        ubuf[0:CONV_HALO, :] = ubuf[ts:ts + CONV_HALO, :]
        pbuf[0:POOL_HALO, :] = pbuf[ts:ts + POOL_HALO, :]

    x = x_ref[0]
    shift1 = mod_ref[0, 0:1, :]
    scale1 = mod_ref[0, 1:2, :]
    gate1 = mod_ref[0, 2:3, :]
    h = _rmsnorm_mod(x, n1g_ref[...], shift1, scale1).astype(BF16)

    def proj(j):
        cols = slice(j * D, (j + 1) * D)
        return jnp.dot(h, win_ref[:, cols], preferred_element_type=F32) + bin_ref[:, cols]

    za = proj(0)
    zb = proj(1)
    ubuf[CONV_HALO:CONV_HALO + ts, :] = za * jax.nn.sigmoid(zb)
    zp = proj(2)
    pbuf[POOL_HALO:POOL_HALO + ts, :] = zp

    lead = CONV_HALO - (CONV_KERNEL - 1)

    def conv_chunk(i, carry):
        r0 = pl.multiple_of(i * CONV_CHUNK, CONV_CHUNK)
        win = ubuf[pl.ds(r0, CONV_CHUNK + CONV_HALO), :]
        acc = jnp.broadcast_to(cb_ref[...], (CONV_CHUNK, D))
        for k in range(CONV_KERNEL):
            acc = acc + cw_ref[k:k + 1, :] * win[lead + k:lead + k + CONV_CHUNK, :]
        cbuf[pl.ds(r0, CONV_CHUNK), :] = acc
        return carry

    lax.fori_loop(0, ts // CONV_CHUNK, conv_chunk, 0)

    u = cbuf[...]
    mu = jnp.mean(u, axis=-1, keepdims=True)
    uc = u - mu
    var = jnp.mean(uc * uc, axis=-1, keepdims=True)
    un = uc * lax.rsqrt(var + EPS) * lng_ref[...] + lnb_ref[...]
    act = (un * jax.nn.sigmoid(un)).astype(BF16)
    conv_out = jnp.dot(act, wco_ref[...], preferred_element_type=F32)

    n_groups = len(POOL_WINDOWS)
    pg = D // n_groups
    frame = lax.broadcasted_iota(jnp.int32, (ts, pg), 0) + (s * ts + 1)
    pool_parts = []
    for g, w in enumerate(POOL_WINDOWS):
        cols = slice(g * pg, (g + 1) * pg)
        tot = pbuf[POOL_HALO:POOL_HALO + ts, cols]
        for d in range(1, w):
            tot = tot + pbuf[POOL_HALO - d:POOL_HALO - d + ts, cols]
        cnt = jnp.minimum(frame, w).astype(F32)
        pooled = tot / cnt - zp[:, cols]
        pool_parts.append(jnp.dot(pooled.astype(BF16), pw_ref[g], preferred_element_type=F32))
    pool_out = jnp.concatenate(pool_parts, axis=-1) * ps_ref[...]

    ga = jax.nn.sigmoid(proj(3))
    gb = jax.nn.sigmoid(proj(4))
    y = (ga * conv_out + gb * pool_out).astype(BF16)
    o_ref[0] = x + gate1 * (jnp.dot(y, wout_ref[...], preferred_element_type=F32) + bout_ref[...])


def _mixer_call(x, mod_l, n1g, w_in, b_in, conv_w, conv_b, ln_g, ln_b, w_co, pool_w, pool_scale, w_out, b_out):
    B, S, D = x.shape
    ts = MIXER_ROWS
    d_in = w_in.shape[1]
    n_groups, pg, _ = pool_w.shape
    const2 = lambda b, s: (0, 0)
    const3 = lambda b, s: (0, 0, 0)
    row = lambda v: v.reshape(1, -1)
    return pl.pallas_call(
        _mixer_kernel,
        grid=(B, S // ts),
        in_specs=[
            pl.BlockSpec((1, ts, D), lambda b, s: (b, s, 0)),
            pl.BlockSpec((1, N_MOD, D), lambda b, s: (b, 0, 0)),
            pl.BlockSpec((1, D), const2),
            pl.BlockSpec((D, d_in), const2),
            pl.BlockSpec((1, d_in), const2),
            pl.BlockSpec((CONV_KERNEL, D), const2),
            pl.BlockSpec((1, D), const2),
            pl.BlockSpec((1, D), const2),
            pl.BlockSpec((1, D), const2),
            pl.BlockSpec((D, D), const2),
            pl.BlockSpec((n_groups, pg, pg), const3),
            pl.BlockSpec((1, D), const2),
            pl.BlockSpec((D, D), const2),
            pl.BlockSpec((1, D), const2),
        ],
        out_specs=pl.BlockSpec((1, ts, D), lambda b, s: (b, s, 0)),
        out_shape=jax.ShapeDtypeStruct((B, S, D), F32),
        scratch_shapes=[
            pltpu.VMEM((CONV_HALO + ts, D), F32),
            pltpu.VMEM((POOL_HALO + ts, D), F32),
            pltpu.VMEM((ts, D), F32),
        ],
        compiler_params=pltpu.CompilerParams(
            dimension_semantics=("arbitrary", "arbitrary"), vmem_limit_bytes=VMEM_LIMIT_BYTES),
        name="mixer",
    )(x, mod_l, row(n1g), w_in.astype(BF16), row(b_in), conv_w, row(conv_b), row(ln_g), row(ln_b),
      w_co.astype(BF16), pool_w.astype(BF16), row(pool_scale), w_out.astype(BF16), row(b_out))


def _bf16_pair_bits(x):
    hi = pltpu.bitcast(x.astype(BF16).astype(F32), jnp.uint32)
    return hi | (hi >> 16)


def _row_as_packed_bf16(ref, hd, row, lanes):
    words = jnp.broadcast_to(ref[hd, row:row + 1, lanes], (SUBLANES, LANES))
    return pltpu.bitcast(words, BF16)


def _top_ranks(s):
    rank = jnp.full(s.shape, float(PEER_TOPK), F32)
    cur = s
    tops = []
    for r in range(PEER_TOPK):
        m = jnp.max(cur, axis=0, keepdims=True)
        hit = cur == m
        rank = jnp.where(hit, float(r), rank)
        cur = jnp.where(hit, NEG_INF, cur)
        tops.append(m)
    return rank, tops


def _route_kernel(x_ref, mod_ref, n2g_ref, wqT_ref, keys_ref, hT_ref, e1_ref, cnt_ref, r2_ref, e2_ref,
                  qT_scr, s_scr):
    tc = x_ref.shape[0]
    shift2 = mod_ref[0, 3:4, :]
    scale2 = mod_ref[0, 4:5, :]
    h = _rmsnorm_mod(x_ref[...], n2g_ref[...], shift2, scale2)
    hT = h.T.astype(BF16)
    hT_ref[...] = hT
    qT_scr[...] = jnp.dot(wqT_ref[...], hT, preferred_element_type=F32)

    row16 = lax.broadcasted_iota(jnp.int32, (PEER_TOPK, LANES), 0)
    row8 = lax.broadcasted_iota(jnp.int32, (SUBLANES, LANES), 0)

    def head_body(hd, carry):
        for p in range(2):
            k0 = pl.multiple_of(hd * (2 * PEER_NKEYS) + p * PEER_NKEYS, PEER_NKEYS)
            q_hp = qT_scr[pl.ds(k0, PEER_NKEYS), :]
            s_scr[p] = jnp.dot(keys_ref[hd, p], q_hp, preferred_element_type=F32,
                               precision=lax.Precision.HIGHEST)

        def chunk_body(c, carry2):
            l0 = pl.multiple_of(c * LANES, LANES)
            s1 = s_scr[0, :, pl.ds(l0, LANES)]
            s2 = s_scr[1, :, pl.ds(l0, LANES)]
            rank1, top1 = _top_ranks(s1)
            rank2, top2 = _top_ranks(s2)
            top2a = jnp.zeros((PEER_TOPK, LANES), F32)
            for r in range(PEER_TOPK):
                top2a = jnp.where(row16 == r, top2[r], top2a)

            slabs = [top1[0] + top2a]
            for i in range(1, PEER_TOPK):
                n_valid = PEER_TOPK // (i + 1)
                slab = top1[i] + top2a[0:SUBLANES]
                slabs.append(jnp.where(row8 < n_valid, slab, NEG_INF))
            cur = list(slabs)
            thr = None
            for r in range(PEER_TOPK):
                m8 = jnp.maximum(cur[0][0:SUBLANES], cur[0][SUBLANES:])
                for sl in cur[1:]:
                    m8 = jnp.maximum(m8, sl)
                thr = jnp.max(m8, axis=0, keepdims=True)
                if r + 1 < PEER_TOPK:
                    cur = [jnp.where(sl == thr, NEG_INF, sl) for sl in cur]

            cmax = top1[0] + top2[0]
            zsum = jnp.zeros((1, LANES), F32)
            cnt = jnp.zeros(s1.shape, F32)
            for i, slab in enumerate(slabs):
                sel = slab >= thr
                n_i = jnp.sum(jnp.where(sel, 1.0, 0.0), axis=0, keepdims=True)
                zsum = zsum + jnp.sum(jnp.where(sel, jnp.exp(slab - cmax), 0.0), axis=0, keepdims=True)
                cnt = jnp.where(rank1 == float(i), n_i, cnt)

            e1_ref[hd, :, pl.ds(l0, LANES)] = _bf16_pair_bits(jnp.exp(s1 - top1[0]))
            cnt_ref[hd, :, pl.ds(l0, LANES)] = _bf16_pair_bits(cnt)
            r2_ref[hd, :, pl.ds(l0, LANES)] = pltpu.bitcast(rank2.astype(BF16), jnp.uint32)
            e2_ref[hd, :, pl.ds(l0, LANES)] = pltpu.bitcast((jnp.exp(s2 - top2[0]) / zsum).astype(BF16), jnp.uint32)
            return carry2

        lax.fori_loop(0, tc // LANES, chunk_body, 0)
        return carry

    lax.fori_loop(0, PEER_HEADS, head_body, 0)


def _route_call(x2d, mod_l, n2g, wqT, keys, seq_len):
    T, D = x2d.shape
    tc = ROUTE_COLS
    HK = wqT.shape[0]
    steps_per_seq = seq_len // tc
    route_shape = jax.ShapeDtypeStruct((PEER_HEADS, PEER_NKEYS, T), jnp.uint32)
    route_bf = jax.ShapeDtypeStruct((PEER_HEADS, PEER_NKEYS // 2, T), jnp.uint32)
    route_spec = pl.BlockSpec((PEER_HEADS, PEER_NKEYS, tc), lambda i: (0, 0, i))
    pair_spec = pl.BlockSpec((PEER_HEADS, PEER_NKEYS // 2, tc), lambda i: (0, 0, i))
    return pl.pallas_call(
        _route_kernel,
        grid=(T // tc,),
        in_specs=[
            pl.BlockSpec((tc, D), lambda i: (i, 0)),
            pl.BlockSpec((1, N_MOD, D), lambda i: (i // steps_per_seq, 0, 0)),
            pl.BlockSpec((1, D), lambda i: (0, 0)),
            pl.BlockSpec((HK, D), lambda i: (0, 0)),
            pl.BlockSpec(keys.shape, lambda i: (0, 0, 0, 0)),
        ],
        out_specs=[pl.BlockSpec((D, tc), lambda i: (0, i)), route_spec, route_spec, pair_spec, pair_spec],
        out_shape=[jax.ShapeDtypeStruct((D, T), BF16), route_shape, route_shape, route_bf, route_bf],
        scratch_shapes=[
            pltpu.VMEM((HK, tc), F32),
            pltpu.VMEM((2, PEER_NKEYS, tc), F32),
        ],
        compiler_params=pltpu.CompilerParams(
            dimension_semantics=("arbitrary",), vmem_limit_bytes=VMEM_LIMIT_BYTES),
        name="peer_route",
    )(x2d, mod_l, n2g.reshape(1, D), wqT, keys)


def _gelu_exact(x):
    return 0.5 * x * (1.0 + lax.erf(x * (1.0 / math.sqrt(2.0))))


def _peer_tile(n, lag, n_tiles, n_expert_tiles):
    t = jnp.clip(n - lag, 0, n_tiles - 1)
    return t // n_expert_tiles, t % n_expert_tiles


def _peer_kernel(hT_ref, e1_ref, cnt_ref, r2_ref, e2_ref, u_ref, vT_ref, x_ref, mod_ref, fg_ref, o_ref,
                 acc, s_scr, p_scr, *, final_norm, n_tiles, n_expert_tiles):
    te = u_ref.shape[0]
    tc = hT_ref.shape[1]
    D = acc.shape[0]
    n = pl.program_id(0)
    slot_s_write = n % 2
    slot_s_read = (n + 1) % 2
    slot_p_write = (n + 1) % 2
    slot_p_read = n % 2
    _, j_val = _peer_tile(n, 2, n_tiles, n_expert_tiles)

    @pl.when(n == 0)
    def _():
        s_scr[...] = jnp.zeros(s_scr.shape, F32)
        p_scr[...] = jnp.zeros(p_scr.shape, BF16)

    @pl.when(j_val == 0)
    def _():
        acc[...] = jnp.zeros(acc.shape, F32)

    a_per_piece = PEER_PIECE // PEER_NKEYS
    n_pieces = te // PEER_PIECE
    n_chunks = tc // LANES

    def score_matmul(q):
        rows = slice(q * PEER_PIECE, (q + 1) * PEER_PIECE)
        s_scr[slot_s_write, rows, :] = jnp.dot(u_ref[rows, :], hT_ref[...], preferred_element_type=F32)

    def value_matmul(q):
        rows = slice(q * (D // n_pieces), (q + 1) * (D // n_pieces))
        acc[rows, :] += jnp.dot(vT_ref[rows, :], p_scr[slot_p_read], preferred_element_type=F32)

    groups = PEER_NKEYS // BF16_ROWS
    zero_bf = jnp.zeros((BF16_ROWS, LANES), BF16)

    def weights_times_gelu(q, c):
        lanes = slice(c * LANES, (c + 1) * LANES)
        ws = [[None] * groups for _ in range(a_per_piece)]
        for hd in range(PEER_HEADS):
            cnt_rows = [_row_as_packed_bf16(cnt_ref, hd, q * a_per_piece + al, lanes) for al in range(a_per_piece)]
            e1_rows = [_row_as_packed_bf16(e1_ref, hd, q * a_per_piece + al, lanes) for al in range(a_per_piece)]
            for g in range(groups):
                words = slice(g * SUBLANES, (g + 1) * SUBLANES)
                r2 = pltpu.bitcast(r2_ref[hd, words, lanes], BF16)
                e2 = pltpu.bitcast(e2_ref[hd, words, lanes], BF16)
                for al in range(a_per_piece):
                    term = jnp.where(r2 < cnt_rows[al], e2, zero_bf) * e1_rows[al]
                    ws[al][g] = term if hd == 0 else ws[al][g] + term
        for al in range(a_per_piece):
            for g in range(groups):
                r0 = q * PEER_PIECE + al * PEER_NKEYS + g * BF16_ROWS
                act = _gelu_exact(s_scr[slot_s_read, r0:r0 + BF16_ROWS, lanes]).astype(BF16)
                p_scr[slot_p_write, r0:r0 + BF16_ROWS, lanes] = ws[al][g] * act

    for q in range(n_pieces):
        for c in range(n_chunks):
            if c == 0:
                score_matmul(q)
            if c == n_chunks // 2:
                value_matmul(q)
            weights_times_gelu(q, c)

    @pl.when(jnp.logical_and(n >= 2, j_val == n_expert_tiles - 1))
    def _():
        gate2 = mod_ref[0, 5:6, :]
        out = x_ref[...] + gate2 * acc[...].T
        if final_norm:
            out = out * lax.rsqrt(jnp.mean(out * out, axis=-1, keepdims=True) + EPS) * fg_ref[...]
        o_ref[...] = out


def _peer_call(hT, e1, cnt, r2, e2, u_bf, vT_bf, x2d, mod_l, final_g, seq_len, final_norm):
    T, D = x2d.shape
    E = u_bf.shape[0]
    tc = PEER_COLS
    te = PEER_EXPERT_TILE
    steps_per_seq = seq_len // tc
    n_expert_tiles = E // te
    n_tiles = (T // tc) * n_expert_tiles
    tile = functools.partial(_peer_tile, n_tiles=n_tiles, n_expert_tiles=n_expert_tiles)
    packed_spec = pl.BlockSpec((PEER_HEADS, PEER_NKEYS // 2, tc), lambda n: (0, 0, tile(n, 1)[0]))
    key_spec = pl.BlockSpec((PEER_HEADS, te // PEER_NKEYS, tc), lambda n: (0, tile(n, 1)[1], tile(n, 1)[0]))
    return pl.pallas_call(
        functools.partial(_peer_kernel, final_norm=final_norm, n_tiles=n_tiles, n_expert_tiles=n_expert_tiles),
        grid=(n_tiles + 2,),
        in_specs=[
            pl.BlockSpec((D, tc), lambda n: (0, tile(n, 0)[0])),
            key_spec, key_spec, packed_spec, packed_spec,
            pl.BlockSpec((te, D), lambda n: (tile(n, 0)[1], 0)),
            pl.BlockSpec((D, te), lambda n: (0, tile(n, 2)[1])),
            pl.BlockSpec((tc, D), lambda n: (tile(n, 2)[0], 0)),
            pl.BlockSpec((1, N_MOD, D), lambda n: (tile(n, 2)[0] // steps_per_seq, 0, 0)),
            pl.BlockSpec((1, D), lambda n: (0, 0)),
        ],
        out_specs=pl.BlockSpec((tc, D), lambda n: (tile(n, 2)[0], 0)),
        out_shape=jax.ShapeDtypeStruct((T, D), F32),
        scratch_shapes=[
            pltpu.VMEM((D, tc), F32),
            pltpu.VMEM((2, te, tc), F32),
            pltpu.VMEM((2, te, tc), BF16),
        ],
        compiler_params=pltpu.CompilerParams(
            dimension_semantics=("arbitrary",), vmem_limit_bytes=VMEM_LIMIT_BYTES),
        name="peer_experts",
    )(hT, e1, cnt, r2, e2, u_bf, vT_bf, x2d, mod_l, final_g.reshape(1, D))


@jax.jit
def _forward(x, c, ada_w, ada_b, norm1_g, w_in, b_in, conv_w, conv_b, conv_ln_g, conv_ln_b, w_conv_out,
             pool_w, pool_scale, w_out, b_out, norm2_g, peer_wq, peer_keys, peer_u, peer_v, final_g):
    B, S, D = x.shape
    L = ada_w.shape[0]
    c_pad = jnp.zeros((SUBLANES, D), F32).at[:B].set(c)
    mod = _ada_call(c_pad, ada_w, ada_b)[:, :B].reshape(L, B, N_MOD, D)
    for l in range(L):
        x = _mixer_call(x, mod[l], norm1_g[l], w_in[l], b_in[l], conv_w[l], conv_b[l], conv_ln_g[l],
                        conv_ln_b[l], w_conv_out[l], pool_w[l], pool_scale[l], w_out[l], b_out[l])
        x2d = x.reshape(B * S, D)
        hT, e1, cnt, r2, e2 = _route_call(x2d, mod[l], norm2_g[l], peer_wq[l].T.astype(BF16), peer_keys[l], S)
        x2d = _peer_call(hT, e1, cnt, r2, e2, peer_u[l].astype(BF16), peer_v[l].T.astype(BF16), x2d, mod[l],
                         final_g, S, final_norm=(l == L - 1))
        x = x2d.reshape(B, S, D)
    return x


def kernel(x, c, ada_w, ada_b, norm1_g, w_in, b_in, conv_w, conv_b, conv_ln_g, conv_ln_b, w_conv_out, pool_w,
           pool_scale, w_out, b_out, norm2_g, peer_wq, peer_keys, peer_u, peer_v, final_g):
    return _forward(x, c, ada_w, ada_b, norm1_g, w_in, b_in, conv_w, conv_b, conv_ln_g, conv_ln_b, w_conv_out,
                    pool_w, pool_scale, w_out, b_out, norm2_g, peer_wq, peer_keys, peer_u, peer_v, final_g)
```

```python
import functools
import math

import jax
import jax.numpy as jnp
from jax import lax
from jax.experimental import pallas as pl
from jax.experimental.pallas import tpu as pltpu

F32 = jnp.float32
BF16 = jnp.bfloat16

EPS = 1e-6
CONV_KERNEL = 31
POOL_WINDOWS = (2, 4, 8, 16)
N_MOD = 6
PEER_HEADS = 8
PEER_NKEYS = 128
PEER_TOPK = 16

LANES = 128
SUBLANES = 8
BF16_ROWS = 16
VMEM_LIMIT_BYTES = 56 * 1024 * 1024

MIXER_ROWS = 512
CONV_HALO = 32
POOL_HALO = 16
CONV_ROWS = 64
CONV_LANES = 128
ROUTE_COLS = 512
ROUTE_CHUNKS_PER_ITER = 4
PEER_COLS = 512
PEER_EXPERT_TILE = 1024
PEER_PIECE = 256

NEG_INF = float("-inf")


def _rmsnorm_mod(x, g, shift, scale):
    y = x * lax.rsqrt(jnp.mean(x * x, axis=-1, keepdims=True) + EPS) * g
    return y * (1.0 + scale) + shift


def _ada_kernel(c_ref, w_ref, b_ref, o_ref):
    c = c_ref[...]
    act = c * jax.nn.sigmoid(c)
    o_ref[0] = jnp.dot(act, w_ref[0], preferred_element_type=F32,
                       precision=lax.Precision.HIGHEST) + b_ref[0]


def _ada_call(c_pad, ada_w, ada_b):
    L, D, ND = ada_w.shape
    rows = c_pad.shape[0]
    return pl.pallas_call(
        _ada_kernel,
        grid=(L, ND // D),
        in_specs=[
            pl.BlockSpec((rows, D), lambda l, n: (0, 0)),
            pl.BlockSpec((1, D, D), lambda l, n: (l, 0, n)),
            pl.BlockSpec((1, 1, D), lambda l, n: (l, 0, n)),
        ],
        out_specs=pl.BlockSpec((1, rows, D), lambda l, n: (l, 0, n)),
        out_shape=jax.ShapeDtypeStruct((L, rows, ND), F32),
        compiler_params=pltpu.CompilerParams(
            dimension_semantics=("arbitrary", "arbitrary"), vmem_limit_bytes=VMEM_LIMIT_BYTES),
        name="ada_mod",
    )(c_pad, ada_w, ada_b.reshape(L, 1, ND))


def _mixer_kernel(x_ref, mod_ref, n1g_ref, win_ref, bin_ref, cw_ref, cb_ref, lng_ref, lnb_ref,
                  wco_ref, pw_ref, ps_ref, wout_ref, bout_ref, o_ref, ubuf, pbuf, cbuf):
    ts, D = cbuf.shape
    s = pl.program_id(1)

    @pl.when(s == 0)
    def _():
        ubuf[0:CONV_HALO, :] = jnp.zeros((CONV_HALO, D), F32)
        pbuf[0:POOL_HALO, :] = jnp.zeros((POOL_HALO, D), F32)

    @pl.when(s > 0)
    def _():
        ubuf[0:CONV_HALO, :] = ubuf[ts:ts + CONV_HALO, :]
        pbuf[0:POOL_HALO, :] = pbuf[ts:ts + POOL_HALO, :]

    x = x_ref[0]
    shift1 = mod_ref[0, 0:1, :]
    scale1 = mod_ref[0, 1:2, :]
    gate1 = mod_ref[0, 2:3, :]
    h = _rmsnorm_mod(x, n1g_ref[...], shift1, scale1).astype(BF16)

    def proj(j):
        cols = slice(j * D, (j + 1) * D)
        return jnp.dot(h, win_ref[:, cols], preferred_element_type=F32) + bin_ref[:, cols]

    za = proj(0)
    zb = proj(1)
    ubuf[CONV_HALO:CONV_HALO + ts, :] = za * jax.nn.sigmoid(zb)
    zp = proj(2)
    pbuf[POOL_HALO:POOL_HALO + ts, :] = zp

    def conv_block(i, carry):
        r0 = pl.multiple_of(i * CONV_ROWS, CONV_ROWS)
        for lb in range(D // CONV_LANES):
            lanes = slice(lb * CONV_LANES, (lb + 1) * CONV_LANES)
            win = ubuf[pl.ds(r0, CONV_ROWS + CONV_HALO), lanes]
            acc = jnp.broadcast_to(cb_ref[:, lanes], (CONV_ROWS, CONV_LANES))
            for r in range(SUBLANES):
                shifted = win if r == 0 else pltpu.roll(win, r, axis=0)
                for q in range(CONV_HALO // SUBLANES):
                    lag = SUBLANES * q + r
                    if lag >= CONV_KERNEL:
                        continue
                    k = CONV_KERNEL - 1 - lag
                    x0 = CONV_HALO - SUBLANES * q
                    acc = acc + cw_ref[k:k + 1, lanes] * shifted[x0:x0 + CONV_ROWS, :]
            cbuf[pl.ds(r0, CONV_ROWS), lanes] = acc
        return carry

    lax.fori_loop(0, ts // CONV_ROWS, conv_block, 0)

    u = cbuf[...]
    mu = jnp.mean(u, axis=-1, keepdims=True)
    uc = u - mu
    var = jnp.mean(uc * uc, axis=-1, keepdims=True)
    un = uc * lax.rsqrt(var + EPS) * lng_ref[...] + lnb_ref[...]
    act = (un * jax.nn.sigmoid(un)).astype(BF16)
    conv_out = jnp.dot(act, wco_ref[...], preferred_element_type=F32)

    n_groups = len(POOL_WINDOWS)
    pg = D // n_groups
    frame = lax.broadcasted_iota(jnp.int32, (ts, pg), 0) + (s * ts + 1)
    pool_parts = []
    for g, w in enumerate(POOL_WINDOWS):
        cols = slice(g * pg, (g + 1) * pg)
        tot = pbuf[POOL_HALO:POOL_HALO + ts, cols]
        for d in range(1, w):
            tot = tot + pbuf[POOL_HALO - d:POOL_HALO - d + ts, cols]
        cnt = jnp.minimum(frame, w).astype(F32)
        pooled = tot / cnt - zp[:, cols]
        pool_parts.append(jnp.dot(pooled.astype(BF16), pw_ref[g], preferred_element_type=F32))
    pool_out = jnp.concatenate(pool_parts, axis=-1) * ps_ref[...]

    ga = jax.nn.sigmoid(proj(3))
    gb = jax.nn.sigmoid(proj(4))
    y = (ga * conv_out + gb * pool_out).astype(BF16)
    o_ref[0] = x + gate1 * (jnp.dot(y, wout_ref[...], preferred_element_type=F32) + bout_ref[...])


def _mixer_call(x, mod_l, n1g, w_in, b_in, conv_w, conv_b, ln_g, ln_b, w_co, pool_w, pool_scale, w_out, b_out):
    B, S, D = x.shape
    ts = MIXER_ROWS
    d_in = w_in.shape[1]
    n_groups, pg, _ = pool_w.shape
    const2 = lambda b, s: (0, 0)
    const3 = lambda b, s: (0, 0, 0)
    row = lambda v: v.reshape(1, -1)
    return pl.pallas_call(
        _mixer_kernel,
        grid=(B, S // ts),
        in_specs=[
            pl.BlockSpec((1, ts, D), lambda b, s: (b, s, 0)),
            pl.BlockSpec((1, N_MOD, D), lambda b, s: (b, 0, 0)),
            pl.BlockSpec((1, D), const2),
            pl.BlockSpec((D, d_in), const2),
            pl.BlockSpec((1, d_in), const2),
            pl.BlockSpec((CONV_KERNEL, D), const2),
            pl.BlockSpec((1, D), const2),
            pl.BlockSpec((1, D), const2),
            pl.BlockSpec((1, D), const2),
            pl.BlockSpec((D, D), const2),
            pl.BlockSpec((n_groups, pg, pg), const3),
            pl.BlockSpec((1, D), const2),
            pl.BlockSpec((D, D), const2),
            pl.BlockSpec((1, D), const2),
        ],
        out_specs=pl.BlockSpec((1, ts, D), lambda b, s: (b, s, 0)),
        out_shape=jax.ShapeDtypeStruct((B, S, D), F32),
        scratch_shapes=[
            pltpu.VMEM((CONV_HALO + ts, D), F32),
            pltpu.VMEM((POOL_HALO + ts, D), F32),
            pltpu.VMEM((ts, D), F32),
        ],
        compiler_params=pltpu.CompilerParams(
            dimension_semantics=("arbitrary", "arbitrary"), vmem_limit_bytes=VMEM_LIMIT_BYTES),
        name="mixer",
    )(x, mod_l, row(n1g), w_in.astype(BF16), row(b_in), conv_w, row(conv_b), row(ln_g), row(ln_b),
      w_co.astype(BF16), pool_w.astype(BF16), row(pool_scale), w_out.astype(BF16), row(b_out))


def _bf16_pair_bits(x):
    hi = pltpu.bitcast(x.astype(BF16).astype(F32), jnp.uint32)
    return hi | (hi >> 16)


def _row_as_packed_bf16(ref, hd, row, lanes):
    words = jnp.broadcast_to(ref[hd, row:row + 1, lanes], (SUBLANES, LANES))
    return pltpu.bitcast(words, BF16)


def _top_ranks(s):
    rank = jnp.full(s.shape, float(PEER_TOPK), F32)
    cur = s
    tops = []
    for r in range(PEER_TOPK):
        m = jnp.max(cur, axis=0, keepdims=True)
        hit = cur == m
        rank = jnp.where(hit, float(r), rank)
        cur = jnp.where(hit, NEG_INF, cur)
        tops.append(m)
    return rank, tops


def _route_kernel(x_ref, mod_ref, n2g_ref, wqT_ref, keys_ref, hT_ref, e1_ref, cnt_ref, r2_ref, e2_ref,
                  qT_scr, s_scr):
    tc = x_ref.shape[0]
    shift2 = mod_ref[0, 3:4, :]
    scale2 = mod_ref[0, 4:5, :]
    h = _rmsnorm_mod(x_ref[...], n2g_ref[...], shift2, scale2)
    hT = h.T.astype(BF16)
    hT_ref[...] = hT
    qT_scr[...] = jnp.dot(wqT_ref[...], hT, preferred_element_type=F32)

    row16 = lax.broadcasted_iota(jnp.int32, (PEER_TOPK, LANES), 0)
    row8 = lax.broadcasted_iota(jnp.int32, (SUBLANES, LANES), 0)

    def head_body(hd, carry):
        for p in range(2):
            k0 = pl.multiple_of(hd * (2 * PEER_NKEYS) + p * PEER_NKEYS, PEER_NKEYS)
            q_hp = qT_scr[pl.ds(k0, PEER_NKEYS), :]
            s_scr[p] = jnp.dot(keys_ref[hd, p], q_hp, preferred_element_type=F32,
                               precision=lax.Precision.HIGHEST)

        def chunk_pair_body(c2, carry2):
            for sub in range(ROUTE_CHUNKS_PER_ITER):
                route_chunk(c2 * ROUTE_CHUNKS_PER_ITER + sub)
            return carry2

        def route_chunk(c):
            l0 = pl.multiple_of(c * LANES, LANES)
            s1 = s_scr[0, :, pl.ds(l0, LANES)]
            s2 = s_scr[1, :, pl.ds(l0, LANES)]
            rank1, top1 = _top_ranks(s1)
            rank2, top2 = _top_ranks(s2)
            top1a = jnp.zeros((PEER_TOPK, LANES), F32)
            top2a = jnp.zeros((PEER_TOPK, LANES), F32)
            for r in range(PEER_TOPK):
                top1a = jnp.where(row16 == r, top1[r], top1a)
                top2a = jnp.where(row16 == r, top2[r], top2a)

            lo2, hi2 = top2a[0:SUBLANES], top2a[SUBLANES:]
            lo1, hi1 = top1a[0:SUBLANES], top1a[SUBLANES:]
            by_j = [top1[0] + lo2, top1[0] + hi2, top1[1] + lo2]
            by_j += [jnp.where(row8 < PEER_TOPK // (i + 1), top1[i] + lo2, NEG_INF) for i in (2, 3, 4)]
            by_i = [jnp.where(row8 >= 5, lo1 + top2[j], NEG_INF) for j in (0, 1)]
            by_i8 = hi1 + top2[0]
            slabs = by_j + by_i + [by_i8]
            cur = list(slabs)
            thr = None
            for r in range(PEER_TOPK):
                m8 = cur[0]
                for sl in cur[1:]:
                    m8 = jnp.maximum(m8, sl)
                thr = jnp.max(m8, axis=0, keepdims=True)
                if r + 1 < PEER_TOPK:
                    cur = [jnp.where(sl == thr, NEG_INF, sl) for sl in cur]

            cmax = top1[0] + top2[0]
            ones = [jnp.where(slab >= thr, 1.0, 0.0) for slab in slabs]
            zsum = jnp.zeros((1, LANES), F32)
            for slab, one in zip(slabs, ones):
                zsum = zsum + jnp.sum(one * jnp.exp(slab - cmax), axis=0, keepdims=True)
            col_sum = lambda v: jnp.sum(v, axis=0, keepdims=True)
            per_i = [col_sum(ones[0]) + col_sum(ones[1])] + [col_sum(ones[k]) for k in (2, 3, 4, 5)]
            tail_lo = ones[6] + ones[7]
            per_i += [tail_lo[i:i + 1] for i in (5, 6, 7)]
            per_i += [ones[8][i:i + 1] for i in range(SUBLANES)]
            cnt = jnp.zeros(s1.shape, F32)
            for i, n_i in enumerate(per_i):
                cnt = jnp.where(rank1 == float(i), n_i, cnt)

            e1_ref[hd, :, pl.ds(l0, LANES)] = _bf16_pair_bits(jnp.exp(s1 - top1[0]))
            cnt_ref[hd, :, pl.ds(l0, LANES)] = _bf16_pair_bits(cnt)
            r2_ref[hd, :, pl.ds(l0, LANES)] = pltpu.bitcast(rank2.astype(BF16), jnp.uint32)
            e2_ref[hd, :, pl.ds(l0, LANES)] = pltpu.bitcast((jnp.exp(s2 - top2[0]) / zsum).astype(BF16), jnp.uint32)

        lax.fori_loop(0, tc // (LANES * ROUTE_CHUNKS_PER_ITER), chunk_pair_body, 0)
        return carry

    lax.fori_loop(0, PEER_HEADS, head_body, 0)


def _route_call(x2d, mod_l, n2g, wqT, keys, seq_len):
    T, D = x2d.shape
    tc = ROUTE_COLS
    HK = wqT.shape[0]
    steps_per_seq = seq_len // tc
    route_shape = jax.ShapeDtypeStruct((PEER_HEADS, PEER_NKEYS, T), jnp.uint32)
    route_bf = jax.ShapeDtypeStruct((PEER_HEADS, PEER_NKEYS // 2, T), jnp.uint32)
    route_spec = pl.BlockSpec((PEER_HEADS, PEER_NKEYS, tc), lambda i: (0, 0, i))
    pair_spec = pl.BlockSpec((PEER_HEADS, PEER_NKEYS // 2, tc), lambda i: (0, 0, i))
    return pl.pallas_call(
        _route_kernel,
        grid=(T // tc,),
        in_specs=[
            pl.BlockSpec((tc, D), lambda i: (i, 0)),
            pl.BlockSpec((1, N_MOD, D), lambda i: (i // steps_per_seq, 0, 0)),
            pl.BlockSpec((1, D), lambda i: (0, 0)),
            pl.BlockSpec((HK, D), lambda i: (0, 0)),
            pl.BlockSpec(keys.shape, lambda i: (0, 0, 0, 0)),
        ],
        out_specs=[pl.BlockSpec((D, tc), lambda i: (0, i)), route_spec, route_spec, pair_spec, pair_spec],
        out_shape=[jax.ShapeDtypeStruct((D, T), BF16), route_shape, route_shape, route_bf, route_bf],
        scratch_shapes=[
            pltpu.VMEM((HK, tc), F32),
            pltpu.VMEM((2, PEER_NKEYS, tc), F32),
        ],
        compiler_params=pltpu.CompilerParams(
            dimension_semantics=("arbitrary",), vmem_limit_bytes=VMEM_LIMIT_BYTES),
        name="peer_route",
    )(x2d, mod_l, n2g.reshape(1, D), wqT, keys)


def _gelu_exact(x):
    return 0.5 * x * (1.0 + lax.erf(x * (1.0 / math.sqrt(2.0))))


def _peer_tile(n, lag, n_tiles, n_expert_tiles):
    t = jnp.clip(n - lag, 0, n_tiles - 1)
    return t // n_expert_tiles, t % n_expert_tiles


def _peer_kernel(hT_ref, e1_ref, cnt_ref, r2_ref, e2_ref, u_ref, vT_ref, x_ref, mod_ref, fg_ref, o_ref,
                 acc, s_scr, p_scr, *, final_norm, n_tiles, n_expert_tiles):
    te = u_ref.shape[0]
    tc = hT_ref.shape[1]
    D = acc.shape[0]
    n = pl.program_id(0)
    slot_s_write = n % 2
    slot_s_read = (n + 1) % 2
    slot_p_write = (n + 1) % 2
    slot_p_read = n % 2
    _, j_val = _peer_tile(n, 2, n_tiles, n_expert_tiles)

    @pl.when(n == 0)
    def _():
        s_scr[...] = jnp.zeros(s_scr.shape, F32)
        p_scr[...] = jnp.zeros(p_scr.shape, BF16)

    @pl.when(j_val == 0)
    def _():
        acc[...] = jnp.zeros(acc.shape, F32)

    a_per_piece = PEER_PIECE // PEER_NKEYS
    n_pieces = te // PEER_PIECE
    n_chunks = tc // LANES

    def score_matmul(q):
        rows = slice(q * PEER_PIECE, (q + 1) * PEER_PIECE)
        s_scr[slot_s_write, rows, :] = jnp.dot(u_ref[rows, :], hT_ref[...], preferred_element_type=F32)

    def value_matmul(q):
        rows = slice(q * (D // n_pieces), (q + 1) * (D // n_pieces))
        acc[rows, :] += jnp.dot(vT_ref[rows, :], p_scr[slot_p_read], preferred_element_type=F32)

    groups = PEER_NKEYS // BF16_ROWS
    zero_bf = jnp.zeros((BF16_ROWS, LANES), BF16)

    def weights_times_gelu(q, c):
        lanes = slice(c * LANES, (c + 1) * LANES)
        ws = [[None] * groups for _ in range(a_per_piece)]
        for hd in range(PEER_HEADS):
            cnt_rows = [_row_as_packed_bf16(cnt_ref, hd, q * a_per_piece + al, lanes) for al in range(a_per_piece)]
            e1_rows = [_row_as_packed_bf16(e1_ref, hd, q * a_per_piece + al, lanes) for al in range(a_per_piece)]
            for g in range(groups):
                words = slice(g * SUBLANES, (g + 1) * SUBLANES)
                r2 = pltpu.bitcast(r2_ref[hd, words, lanes], BF16)
                e2 = pltpu.bitcast(e2_ref[hd, words, lanes], BF16)
                for al in range(a_per_piece):
                    term = jnp.where(r2 < cnt_rows[al], e2, zero_bf) * e1_rows[al]
                    ws[al][g] = term if hd == 0 else ws[al][g] + term
        for al in range(a_per_piece):
            for g in range(groups):
                r0 = q * PEER_PIECE + al * PEER_NKEYS + g * BF16_ROWS
                act = _gelu_exact(s_scr[slot_s_read, r0:r0 + BF16_ROWS, lanes]).astype(BF16)
                p_scr[slot_p_write, r0:r0 + BF16_ROWS, lanes] = ws[al][g] * act

    for q in range(n_pieces):
        for c in range(n_chunks):
            if c == 0:
                score_matmul(q)
            if c == n_chunks // 2:
                value_matmul(q)
            weights_times_gelu(q, c)

    @pl.when(jnp.logical_and(n >= 2, j_val == n_expert_tiles - 1))
    def _():
        gate2 = mod_ref[0, 5:6, :]
        out = x_ref[...] + gate2 * acc[...].T
        if final_norm:
            out = out * lax.rsqrt(jnp.mean(out * out, axis=-1, keepdims=True) + EPS) * fg_ref[...]
        o_ref[...] = out


def _peer_call(hT, e1, cnt, r2, e2, u_bf, vT_bf, x2d, mod_l, final_g, seq_len, final_norm):
    T, D = x2d.shape
    E = u_bf.shape[0]
    tc = PEER_COLS
    te = PEER_EXPERT_TILE
    steps_per_seq = seq_len // tc
    n_expert_tiles = E // te
    n_tiles = (T // tc) * n_expert_tiles
    tile = functools.partial(_peer_tile, n_tiles=n_tiles, n_expert_tiles=n_expert_tiles)
    packed_spec = pl.BlockSpec((PEER_HEADS, PEER_NKEYS // 2, tc), lambda n: (0, 0, tile(n, 1)[0]))
    key_spec = pl.BlockSpec((PEER_HEADS, te // PEER_NKEYS, tc), lambda n: (0, tile(n, 1)[1], tile(n, 1)[0]))
    return pl.pallas_call(
        functools.partial(_peer_kernel, final_norm=final_norm, n_tiles=n_tiles, n_expert_tiles=n_expert_tiles),
        grid=(n_tiles + 2,),
        in_specs=[
            pl.BlockSpec((D, tc), lambda n: (0, tile(n, 0)[0])),
            key_spec, key_spec, packed_spec, packed_spec,
            pl.BlockSpec((te, D), lambda n: (tile(n, 0)[1], 0)),
            pl.BlockSpec((D, te), lambda n: (0, tile(n, 2)[1])),
            pl.BlockSpec((tc, D), lambda n: (tile(n, 2)[0], 0)),
            pl.BlockSpec((1, N_MOD, D), lambda n: (tile(n, 2)[0] // steps_per_seq, 0, 0)),
            pl.BlockSpec((1, D), lambda n: (0, 0)),
        ],
        out_specs=pl.BlockSpec((tc, D), lambda n: (tile(n, 2)[0], 0)),
        out_shape=jax.ShapeDtypeStruct((T, D), F32),
        scratch_shapes=[
            pltpu.VMEM((D, tc), F32),
            pltpu.VMEM((2, te, tc), F32),
            pltpu.VMEM((2, te, tc), BF16),
        ],
        compiler_params=pltpu.CompilerParams(
            dimension_semantics=("arbitrary",), vmem_limit_bytes=VMEM_LIMIT_BYTES),
        name="peer_experts",
    )(hT, e1, cnt, r2, e2, u_bf, vT_bf, x2d, mod_l, final_g.reshape(1, D))


@jax.jit
def _forward(x, c, ada_w, ada_b, norm1_g, w_in, b_in, conv_w, conv_b, conv_ln_g, conv_ln_b, w_conv_out,
             pool_w, pool_scale, w_out, b_out, norm2_g, peer_wq, peer_keys, peer_u, peer_v, final_g):
    B, S, D = x.shape
    L = ada_w.shape[0]
    c_pad = jnp.zeros((SUBLANES, D), F32).at[:B].set(c)
    mod = _ada_call(c_pad, ada_w, ada_b)[:, :B].reshape(L, B, N_MOD, D)
    for l in range(L):
        x = _mixer_call(x, mod[l], norm1_g[l], w_in[l], b_in[l], conv_w[l], conv_b[l], conv_ln_g[l],
                        conv_ln_b[l], w_conv_out[l], pool_w[l], pool_scale[l], w_out[l], b_out[l])
        x2d = x.reshape(B * S, D)
        hT, e1, cnt, r2, e2 = _route_call(x2d, mod[l], norm2_g[l], peer_wq[l].T.astype(BF16), peer_keys[l], S)
        x2d = _peer_call(hT, e1, cnt, r2, e2, peer_u[l].astype(BF16), peer_v[l].T.astype(BF16), x2d, mod[l],
                         final_g, S, final_norm=(l == L - 1))
        x = x2d.reshape(B, S, D)
    return x


def kernel(x, c, ada_w, ada_b, norm1_g, w_in, b_in, conv_w, conv_b, conv_ln_g, conv_ln_b, w_conv_out, pool_w,
           pool_scale, w_out, b_out, norm2_g, peer_wq, peer_keys, peer_u, peer_v, final_g):
    return _forward(x, c, ada_w, ada_b, norm1_g, w_in, b_in, conv_w, conv_b, conv_ln_g, conv_ln_b, w_conv_out,
                    pool_w, pool_scale, w_out, b_out, norm2_g, peer_wq, peer_keys, peer_u, peer_v, final_g)
```

```python
import functools
import math

import jax
import jax.numpy as jnp
from jax import lax
from jax.experimental import pallas as pl
from jax.experimental.pallas import tpu as pltpu

F32 = jnp.float32
BF16 = jnp.bfloat16

EPS = 1e-6
CONV_KERNEL = 31
POOL_WINDOWS = (2, 4, 8, 16)
N_MOD = 6
PEER_HEADS = 8
PEER_NKEYS = 128
PEER_TOPK = 16

LANES = 128
SUBLANES = 8
BF16_ROWS = 16
VMEM_LIMIT_BYTES = 56 * 1024 * 1024

MIXER_ROWS = 512
CONV_HALO = 32
POOL_HALO = 16
CONV_ROWS = 64
CONV_LANES = 128
ROUTE_COLS = 512
ROUTE_CHUNKS_PER_ITER = 4
PEER_COLS = 512
PEER_EXPERT_TILE = 1024
PEER_PIECE = 256

NEG_INF = float("-inf")


def _rmsnorm_mod(x, g, shift, scale):
    y = x * lax.rsqrt(jnp.mean(x * x, axis=-1, keepdims=True) + EPS) * g
    return y * (1.0 + scale) + shift


def _ada_kernel(c_ref, w_ref, b_ref, o_ref):
    c = c_ref[...]
    act = c * jax.nn.sigmoid(c)
    o_ref[0] = jnp.dot(act, w_ref[0], preferred_element_type=F32,
                       precision=lax.Precision.HIGHEST) + b_ref[0]


def _ada_call(c_pad, ada_w, ada_b):
    L, D, ND = ada_w.shape
    rows = c_pad.shape[0]
    return pl.pallas_call(
        _ada_kernel,
        grid=(L, ND // D),
        in_specs=[
            pl.BlockSpec((rows, D), lambda l, n: (0, 0)),
            pl.BlockSpec((1, D, D), lambda l, n: (l, 0, n)),
            pl.BlockSpec((1, 1, D), lambda l, n: (l, 0, n)),
        ],
        out_specs=pl.BlockSpec((1, rows, D), lambda l, n: (l, 0, n)),
        out_shape=jax.ShapeDtypeStruct((L, rows, ND), F32),
        compiler_params=pltpu.CompilerParams(
            dimension_semantics=("arbitrary", "arbitrary"), vmem_limit_bytes=VMEM_LIMIT_BYTES),
        name="ada_mod",
    )(c_pad, ada_w, ada_b.reshape(L, 1, ND))


def _mixer_kernel(x_ref, mod_ref, n1g_ref, win_ref, bin_ref, cw_ref, cb_ref, lng_ref, lnb_ref,
                  wco_ref, pw_ref, ps_ref, wout_ref, bout_ref, o_ref, ubuf, pbuf, cbuf):
    ts, D = cbuf.shape
    s = pl.program_id(1)

    @pl.when(s == 0)
    def _():
        ubuf[0:CONV_HALO, :] = jnp.zeros((CONV_HALO, D), F32)
        pbuf[0:POOL_HALO, :] = jnp.zeros((POOL_HALO, D), F32)

    @pl.when(s > 0)
    def _():
        ubuf[0:CONV_HALO, :] = ubuf[ts:ts + CONV_HALO, :]
        pbuf[0:POOL_HALO, :] = pbuf[ts:ts + POOL_HALO, :]

    x = x_ref[0]
    shift1 = mod_ref[0, 0:1, :]
    scale1 = mod_ref[0, 1:2, :]
    gate1 = mod_ref[0, 2:3, :]
    h = _rmsnorm_mod(x, n1g_ref[...], shift1, scale1).astype(BF16)

    def proj(j):
        cols = slice(j * D, (j + 1) * D)
        return jnp.dot(h, win_ref[:, cols], preferred_element_type=F32) + bin_ref[:, cols]

    za = proj(0)
    zb = proj(1)
    ubuf[CONV_HALO:CONV_HALO + ts, :] = za * jax.nn.sigmoid(zb)
    zp = proj(2)
    pbuf[POOL_HALO:POOL_HALO + ts, :] = zp

    def conv_block(i, carry):
        r0 = pl.multiple_of(i * CONV_ROWS, CONV_ROWS)
        for lb in range(D // CONV_LANES):
            lanes = slice(lb * CONV_LANES, (lb + 1) * CONV_LANES)
            win = ubuf[pl.ds(r0, CONV_ROWS + CONV_HALO), lanes]
            acc = jnp.broadcast_to(cb_ref[:, lanes], (CONV_ROWS, CONV_LANES))
            for r in range(SUBLANES):
                shifted = win if r == 0 else pltpu.roll(win, r, axis=0)
                for q in range(CONV_HALO // SUBLANES):
                    lag = SUBLANES * q + r
                    if lag >= CONV_KERNEL:
                        continue
                    k = CONV_KERNEL - 1 - lag
                    x0 = CONV_HALO - SUBLANES * q
                    acc = acc + cw_ref[k:k + 1, lanes] * shifted[x0:x0 + CONV_ROWS, :]
            cbuf[pl.ds(r0, CONV_ROWS), lanes] = acc
        return carry

    lax.fori_loop(0, ts // CONV_ROWS, conv_block, 0)

    u = cbuf[...]
    mu = jnp.mean(u, axis=-1, keepdims=True)
    uc = u - mu
    var = jnp.mean(uc * uc, axis=-1, keepdims=True)
    un = uc * lax.rsqrt(var + EPS) * lng_ref[...] + lnb_ref[...]
    act = (un * jax.nn.sigmoid(un)).astype(BF16)
    conv_out = jnp.dot(act, wco_ref[...], preferred_element_type=F32)

    n_groups = len(POOL_WINDOWS)
    pg = D // n_groups
    frame = lax.broadcasted_iota(jnp.int32, (ts, pg), 0) + (s * ts + 1)
    pool_parts = []
    for g, w in enumerate(POOL_WINDOWS):
        cols = slice(g * pg, (g + 1) * pg)
        tot = pbuf[POOL_HALO:POOL_HALO + ts, cols]
        for d in range(1, w):
            tot = tot + pbuf[POOL_HALO - d:POOL_HALO - d + ts, cols]
        cnt = jnp.minimum(frame, w).astype(F32)
        pooled = tot / cnt - zp[:, cols]
        pool_parts.append(jnp.dot(pooled.astype(BF16), pw_ref[g], preferred_element_type=F32))
    pool_out = jnp.concatenate(pool_parts, axis=-1) * ps_ref[...]

    ga = jax.nn.sigmoid(proj(3))
    gb = jax.nn.sigmoid(proj(4))
    y = (ga * conv_out + gb * pool_out).astype(BF16)
    o_ref[0] = x + gate1 * (jnp.dot(y, wout_ref[...], preferred_element_type=F32) + bout_ref[...])


def _mixer_call(x, mod_l, n1g, w_in, b_in, conv_w, conv_b, ln_g, ln_b, w_co, pool_w, pool_scale, w_out, b_out):
    B, S, D = x.shape
    ts = MIXER_ROWS
    d_in = w_in.shape[1]
    n_groups, pg, _ = pool_w.shape
    const2 = lambda b, s: (0, 0)
    const3 = lambda b, s: (0, 0, 0)
    row = lambda v: v.reshape(1, -1)
    return pl.pallas_call(
        _mixer_kernel,
        grid=(B, S // ts),
        in_specs=[
            pl.BlockSpec((1, ts, D), lambda b, s: (b, s, 0)),
            pl.BlockSpec((1, N_MOD, D), lambda b, s: (b, 0, 0)),
            pl.BlockSpec((1, D), const2),
            pl.BlockSpec((D, d_in), const2),
            pl.BlockSpec((1, d_in), const2),
            pl.BlockSpec((CONV_KERNEL, D), const2),
            pl.BlockSpec((1, D), const2),
            pl.BlockSpec((1, D), const2),
            pl.BlockSpec((1, D), const2),
            pl.BlockSpec((D, D), const2),
            pl.BlockSpec((n_groups, pg, pg), const3),
            pl.BlockSpec((1, D), const2),
            pl.BlockSpec((D, D), const2),
            pl.BlockSpec((1, D), const2),
        ],
        out_specs=pl.BlockSpec((1, ts, D), lambda b, s: (b, s, 0)),
        out_shape=jax.ShapeDtypeStruct((B, S, D), F32),
        scratch_shapes=[
            pltpu.VMEM((CONV_HALO + ts, D), F32),
            pltpu.VMEM((POOL_HALO + ts, D), F32),
            pltpu.VMEM((ts, D), F32),
        ],
        compiler_params=pltpu.CompilerParams(
            dimension_semantics=("arbitrary", "arbitrary"), vmem_limit_bytes=VMEM_LIMIT_BYTES),
        name="mixer",
    )(x, mod_l, row(n1g), w_in.astype(BF16), row(b_in), conv_w, row(conv_b), row(ln_g), row(ln_b),
      w_co.astype(BF16), pool_w.astype(BF16), row(pool_scale), w_out.astype(BF16), row(b_out))


def _bf16_pair_bits(x):
    hi = pltpu.bitcast(x.astype(BF16).astype(F32), jnp.uint32)
    return hi | (hi >> 16)


def _row_as_packed_bf16(ref, hd, row, lanes):
    words = jnp.broadcast_to(ref[hd, row:row + 1, lanes], (SUBLANES, LANES))
    return pltpu.bitcast(words, BF16)


def _top_ranks(s):
    rank = jnp.full(s.shape, float(PEER_TOPK), F32)
    cur = s
    tops = []
    for r in range(PEER_TOPK):
        m = jnp.max(cur, axis=0, keepdims=True)
        hit = cur == m
        rank = jnp.where(hit, float(r), rank)
        cur = jnp.where(hit, NEG_INF, cur)
        tops.append(m)
    return rank, tops


def _route_kernel(x_ref, mod_ref, n2g_ref, wqT_ref, keys_ref, hT_ref, e1_ref, cnt_ref, r2_ref, e2_ref,
                  qT_scr, s_scr):
    tc = x_ref.shape[0]
    shift2 = mod_ref[0, 3:4, :]
    scale2 = mod_ref[0, 4:5, :]
    h = _rmsnorm_mod(x_ref[...], n2g_ref[...], shift2, scale2)
    hT = h.T.astype(BF16)
    hT_ref[...] = hT
    qT_scr[...] = jnp.dot(wqT_ref[...], hT, preferred_element_type=F32)

    row16 = lax.broadcasted_iota(jnp.int32, (PEER_TOPK, LANES), 0)
    row8 = lax.broadcasted_iota(jnp.int32, (SUBLANES, LANES), 0)

    def head_body(hd, carry):
        for p in range(2):
            k0 = pl.multiple_of(hd * (2 * PEER_NKEYS) + p * PEER_NKEYS, PEER_NKEYS)
            q_hp = qT_scr[pl.ds(k0, PEER_NKEYS), :]
            s_scr[p] = jnp.dot(keys_ref[hd, p], q_hp, preferred_element_type=F32,
                               precision=lax.Precision.HIGHEST)

        def chunk_pair_body(c2, carry2):
            for sub in range(ROUTE_CHUNKS_PER_ITER):
                route_chunk(c2 * ROUTE_CHUNKS_PER_ITER + sub)
            return carry2

        def route_chunk(c):
            l0 = pl.multiple_of(c * LANES, LANES)
            s1 = s_scr[0, :, pl.ds(l0, LANES)]
            s2 = s_scr[1, :, pl.ds(l0, LANES)]
            rank1, top1 = _top_ranks(s1)
            rank2, top2 = _top_ranks(s2)
            top1a = jnp.zeros((PEER_TOPK, LANES), F32)
            top2a = jnp.zeros((PEER_TOPK, LANES), F32)
            for r in range(PEER_TOPK):
                top1a = jnp.where(row16 == r, top1[r], top1a)
                top2a = jnp.where(row16 == r, top2[r], top2a)

            lo2, hi2 = top2a[0:SUBLANES], top2a[SUBLANES:]
            lo1, hi1 = top1a[0:SUBLANES], top1a[SUBLANES:]
            by_j = [top1[0] + lo2, top1[0] + hi2, top1[1] + lo2]
            by_j += [jnp.where(row8 < PEER_TOPK // (i + 1), top1[i] + lo2, NEG_INF) for i in (2, 3, 4)]
            by_i = [jnp.where(row8 >= 5, lo1 + top2[j], NEG_INF) for j in (0, 1)]
            by_i8 = hi1 + top2[0]
            slabs = by_j + by_i + [by_i8]
            cur = list(slabs)
            thr = None
            for r in range(PEER_TOPK):
                m8 = cur[0]
                for sl in cur[1:]:
                    m8 = jnp.maximum(m8, sl)
                thr = jnp.max(m8, axis=0, keepdims=True)
                if r + 1 < PEER_TOPK:
                    cur = [jnp.where(sl == thr, NEG_INF, sl) for sl in cur]

            cmax = top1[0] + top2[0]
            ones = [jnp.where(slab >= thr, 1.0, 0.0) for slab in slabs]
            zsum = jnp.zeros((1, LANES), F32)
            for slab, one in zip(slabs, ones):
                zsum = zsum + jnp.sum(one * jnp.exp(slab - cmax), axis=0, keepdims=True)
            col_sum = lambda v: jnp.sum(v, axis=0, keepdims=True)
            per_i = [col_sum(ones[0]) + col_sum(ones[1])] + [col_sum(ones[k]) for k in (2, 3, 4, 5)]
            tail_lo = ones[6] + ones[7]
            per_i += [tail_lo[i:i + 1] for i in (5, 6, 7)]
            per_i += [ones[8][i:i + 1] for i in range(SUBLANES)]
            cnt = jnp.zeros(s1.shape, F32)
            for i, n_i in enumerate(per_i):
                cnt = jnp.where(rank1 == float(i), n_i, cnt)

            e1_ref[hd, :, pl.ds(l0, LANES)] = _bf16_pair_bits(jnp.exp(s1 - top1[0]))
            cnt_ref[hd, :, pl.ds(l0, LANES)] = _bf16_pair_bits(cnt)
            r2_ref[hd, :, pl.ds(l0, LANES)] = pltpu.bitcast(rank2.astype(BF16), jnp.uint32)
            e2_ref[hd, :, pl.ds(l0, LANES)] = pltpu.bitcast((jnp.exp(s2 - top2[0]) / zsum).astype(BF16), jnp.uint32)

        lax.fori_loop(0, tc // (LANES * ROUTE_CHUNKS_PER_ITER), chunk_pair_body, 0)
        return carry

    lax.fori_loop(0, PEER_HEADS, head_body, 0)


def _route_call(x2d, mod_l, n2g, wqT, keys, seq_len):
    T, D = x2d.shape
    tc = ROUTE_COLS
    HK = wqT.shape[0]
    steps_per_seq = seq_len // tc
    route_shape = jax.ShapeDtypeStruct((PEER_HEADS, PEER_NKEYS, T), jnp.uint32)
    route_bf = jax.ShapeDtypeStruct((PEER_HEADS, PEER_NKEYS // 2, T), jnp.uint32)
    route_spec = pl.BlockSpec((PEER_HEADS, PEER_NKEYS, tc), lambda i: (0, 0, i))
    pair_spec = pl.BlockSpec((PEER_HEADS, PEER_NKEYS // 2, tc), lambda i: (0, 0, i))
    return pl.pallas_call(
        _route_kernel,
        grid=(T // tc,),
        in_specs=[
            pl.BlockSpec((tc, D), lambda i: (i, 0)),
            pl.BlockSpec((1, N_MOD, D), lambda i: (i // steps_per_seq, 0, 0)),
            pl.BlockSpec((1, D), lambda i: (0, 0)),
            pl.BlockSpec((HK, D), lambda i: (0, 0)),
            pl.BlockSpec(keys.shape, lambda i: (0, 0, 0, 0)),
        ],
        out_specs=[pl.BlockSpec((D, tc), lambda i: (0, i)), route_spec, route_spec, pair_spec, pair_spec],
        out_shape=[jax.ShapeDtypeStruct((D, T), BF16), route_shape, route_shape, route_bf, route_bf],
        scratch_shapes=[
            pltpu.VMEM((HK, tc), F32),
            pltpu.VMEM((2, PEER_NKEYS, tc), F32),
        ],
        compiler_params=pltpu.CompilerParams(
            dimension_semantics=("arbitrary",), vmem_limit_bytes=VMEM_LIMIT_BYTES),
        name="peer_route",
    )(x2d, mod_l, n2g.reshape(1, D), wqT, keys)


def _gelu_exact(x):
    return 0.5 * x * (1.0 + lax.erf(x * (1.0 / math.sqrt(2.0))))


def _peer_tile(n, lag, n_tiles, n_expert_tiles):
    t = jnp.clip(n - lag, 0, n_tiles - 1)
    return t // n_expert_tiles, t % n_expert_tiles


def _peer_kernel(hT_ref, e1_ref, cnt_ref, r2_ref, e2_ref, u_ref, vT_ref, x_ref, mod_ref, fg_ref, o_ref,
                 acc, p_even, p_odd, *, final_norm, n_tiles, n_expert_tiles):
    te = vT_ref.shape[1]
    tc = hT_ref.shape[1]
    n = pl.program_id(0)
    _, j_val = _peer_tile(n, 1, n_tiles, n_expert_tiles)

    @pl.when(n == 0)
    def _():
        for ref in (p_even, p_odd):
            ref[...] = jnp.zeros(ref.shape, ref.dtype)

    @pl.when(j_val == 0)
    def _():
        acc[...] = jnp.zeros(acc.shape, F32)

    a_per_piece = PEER_PIECE // PEER_NKEYS
    n_pieces = te // PEER_PIECE
    n_chunks = tc // LANES
    groups = PEER_NKEYS // BF16_ROWS
    zero_bf = jnp.zeros((BF16_ROWS, LANES), BF16)

    def step(p_write, p_read):
        acc[...] += jnp.dot(pltpu.bitcast(vT_ref[...], BF16), p_read[...], preferred_element_type=F32)
        for q in range(n_pieces):
            words_q = slice(q * PEER_PIECE // 2, (q + 1) * PEER_PIECE // 2)
            pre = jnp.dot(pltpu.bitcast(u_ref[words_q, :], BF16), hT_ref[...],
                          preferred_element_type=F32)
            for c in range(n_chunks):
                lanes = slice(c * LANES, (c + 1) * LANES)
                ws = [[None] * groups for _ in range(a_per_piece)]
                for hd in range(PEER_HEADS):
                    cnt_rows = [_row_as_packed_bf16(cnt_ref, hd, q * a_per_piece + al, lanes)
                                for al in range(a_per_piece)]
                    e1_rows = [_row_as_packed_bf16(e1_ref, hd, q * a_per_piece + al, lanes)
                               for al in range(a_per_piece)]
                    for g in range(groups):
                        words = slice(g * SUBLANES, (g + 1) * SUBLANES)
                        r2 = pltpu.bitcast(r2_ref[hd, words, lanes], BF16)
                        e2 = pltpu.bitcast(e2_ref[hd, words, lanes], BF16)
                        for al in range(a_per_piece):
                            term = jnp.where(r2 < cnt_rows[al], e2, zero_bf) * e1_rows[al]
                            ws[al][g] = term if hd == 0 else ws[al][g] + term
                for al in range(a_per_piece):
                    for g in range(groups):
                        rl = al * PEER_NKEYS + g * BF16_ROWS
                        act = _gelu_exact(pre[rl:rl + BF16_ROWS, lanes]).astype(BF16)
                        p_write[q * PEER_PIECE + rl:q * PEER_PIECE + rl + BF16_ROWS, lanes] = ws[al][g] * act

    @pl.when(n % 2 == 0)
    def _():
        step(p_even, p_odd)

    @pl.when(n % 2 == 1)
    def _():
        step(p_odd, p_even)

    @pl.when(jnp.logical_and(n >= 1, j_val == n_expert_tiles - 1))
    def _():
        gate2 = mod_ref[0, 5:6, :]
        out = x_ref[...] + gate2 * acc[...].T
        if final_norm:
            out = out * lax.rsqrt(jnp.mean(out * out, axis=-1, keepdims=True) + EPS) * fg_ref[...]
        o_ref[...] = out


def _peer_call(hT, e1, cnt, r2, e2, u_words, vT_words, x2d, mod_l, final_g, seq_len, final_norm):
    T, D = x2d.shape
    E = vT_words.shape[1]
    tc = PEER_COLS
    te = PEER_EXPERT_TILE
    steps_per_seq = seq_len // tc
    n_expert_tiles = E // te
    n_tiles = (T // tc) * n_expert_tiles
    tile = functools.partial(_peer_tile, n_tiles=n_tiles, n_expert_tiles=n_expert_tiles)
    packed_spec = pl.BlockSpec((PEER_HEADS, PEER_NKEYS // 2, tc), lambda n: (0, 0, tile(n, 0)[0]))
    key_spec = pl.BlockSpec((PEER_HEADS, te // PEER_NKEYS, tc), lambda n: (0, tile(n, 0)[1], tile(n, 0)[0]))
    return pl.pallas_call(
        functools.partial(_peer_kernel, final_norm=final_norm, n_tiles=n_tiles, n_expert_tiles=n_expert_tiles),
        grid=(n_tiles + 1,),
        in_specs=[
            pl.BlockSpec((D, tc), lambda n: (0, tile(n, 0)[0])),
            key_spec, key_spec, packed_spec, packed_spec,
            pl.BlockSpec((te // 2, D), lambda n: (tile(n, 0)[1], 0)),
            pl.BlockSpec((D // 2, te), lambda n: (0, tile(n, 1)[1])),
            pl.BlockSpec((tc, D), lambda n: (tile(n, 1)[0], 0)),
            pl.BlockSpec((1, N_MOD, D), lambda n: (tile(n, 1)[0] // steps_per_seq, 0, 0)),
            pl.BlockSpec((1, D), lambda n: (0, 0)),
        ],
        out_specs=pl.BlockSpec((tc, D), lambda n: (tile(n, 1)[0], 0)),
        out_shape=jax.ShapeDtypeStruct((T, D), F32),
        scratch_shapes=[
            pltpu.VMEM((D, tc), F32),
            pltpu.VMEM((te, tc), BF16),
            pltpu.VMEM((te, tc), BF16),
        ],
        compiler_params=pltpu.CompilerParams(
            dimension_semantics=("arbitrary",), vmem_limit_bytes=VMEM_LIMIT_BYTES),
        name="peer_experts",
    )(hT, e1, cnt, r2, e2, u_words, vT_words, x2d, mod_l, final_g.reshape(1, D))


def _pack_row_pairs(w):
    rows, cols = w.shape
    return lax.bitcast_convert_type(w.reshape(rows // 2, 2, cols).transpose(0, 2, 1), jnp.uint32)


@jax.jit
def _forward(x, c, ada_w, ada_b, norm1_g, w_in, b_in, conv_w, conv_b, conv_ln_g, conv_ln_b, w_conv_out,
             pool_w, pool_scale, w_out, b_out, norm2_g, peer_wq, peer_keys, peer_u, peer_v, final_g):
    B, S, D = x.shape
    L = ada_w.shape[0]
    c_pad = jnp.zeros((SUBLANES, D), F32).at[:B].set(c)
    mod = _ada_call(c_pad, ada_w, ada_b)[:, :B].reshape(L, B, N_MOD, D)
    for l in range(L):
        x = _mixer_call(x, mod[l], norm1_g[l], w_in[l], b_in[l], conv_w[l], conv_b[l], conv_ln_g[l],
                        conv_ln_b[l], w_conv_out[l], pool_w[l], pool_scale[l], w_out[l], b_out[l])
        x2d = x.reshape(B * S, D)
        hT, e1, cnt, r2, e2 = _route_call(x2d, mod[l], norm2_g[l], peer_wq[l].T.astype(BF16), peer_keys[l], S)
        x2d = _peer_call(hT, e1, cnt, r2, e2, _pack_row_pairs(peer_u[l].astype(BF16)),
                         _pack_row_pairs(peer_v[l].T.astype(BF16)), x2d, mod[l],
                         final_g, S, final_norm=(l == L - 1))
        x = x2d.reshape(B, S, D)
    return x


def kernel(x, c, ada_w, ada_b, norm1_g, w_in, b_in, conv_w, conv_b, conv_ln_g, conv_ln_b, w_conv_out, pool_w,
           pool_scale, w_out, b_out, norm2_g, peer_wq, peer_keys, peer_u, peer_v, final_g):
    return _forward(x, c, ada_w, ada_b, norm1_g, w_in, b_in, conv_w, conv_b, conv_ln_g, conv_ln_b, w_conv_out,
                    pool_w, pool_scale, w_out, b_out, norm2_g, peer_wq, peer_keys, peer_u, peer_v, final_g)
```

```python
import functools
import math

import jax
import jax.numpy as jnp
from jax import lax
from jax.experimental import pallas as pl
from jax.experimental.pallas import tpu as pltpu

F32 = jnp.float32
BF16 = jnp.bfloat16

EPS = 1e-6
CONV_KERNEL = 31
POOL_WINDOWS = (2, 4, 8, 16)
N_MOD = 6
PEER_HEADS = 8
PEER_NKEYS = 128
PEER_TOPK = 16

LANES = 128
SUBLANES = 8
BF16_ROWS = 16
VMEM_LIMIT_BYTES = 56 * 1024 * 1024

MIXER_ROWS = 512
CONV_HALO = 32
POOL_HALO = 16
CONV_ROWS = 64
CONV_LANES = 128
ROUTE_COLS = 512
ROUTE_CHUNKS_PER_ITER = 4
PEER_COLS = 512
PEER_EXPERT_TILE = 1024
PEER_PIECE = 256
PACK_ROWS = 1024

NEG_INF = float("-inf")


def _rmsnorm_mod(x, g, shift, scale):
    y = x * lax.rsqrt(jnp.mean(x * x, axis=-1, keepdims=True) + EPS) * g
    return y * (1.0 + scale) + shift


def _ada_kernel(c_ref, w_ref, b_ref, o_ref):
    c = c_ref[...]
    act = c * jax.nn.sigmoid(c)
    o_ref[0] = jnp.dot(act, w_ref[0], preferred_element_type=F32,
                       precision=lax.Precision.HIGHEST) + b_ref[0]


def _ada_call(c_pad, ada_w, ada_b):
    L, D, ND = ada_w.shape
    rows = c_pad.shape[0]
    return pl.pallas_call(
        _ada_kernel,
        grid=(L, ND // D),
        in_specs=[
            pl.BlockSpec((rows, D), lambda l, n: (0, 0)),
            pl.BlockSpec((1, D, D), lambda l, n: (l, 0, n)),
            pl.BlockSpec((1, 1, D), lambda l, n: (l, 0, n)),
        ],
        out_specs=pl.BlockSpec((1, rows, D), lambda l, n: (l, 0, n)),
        out_shape=jax.ShapeDtypeStruct((L, rows, ND), F32),
        compiler_params=pltpu.CompilerParams(
            dimension_semantics=("arbitrary", "arbitrary"), vmem_limit_bytes=VMEM_LIMIT_BYTES),
        name="ada_mod",
    )(c_pad, ada_w, ada_b.reshape(L, 1, ND))


def _mixer_kernel(x_ref, mod_ref, n1g_ref, win_ref, bin_ref, cw_ref, cb_ref, lng_ref, lnb_ref,
                  wco_ref, pw_ref, ps_ref, wout_ref, bout_ref, o_ref, ubuf, pbuf, cbuf):
    ts, D = cbuf.shape
    s = pl.program_id(1)

    @pl.when(s == 0)
    def _():
        ubuf[0:CONV_HALO, :] = jnp.zeros((CONV_HALO, D), F32)
        pbuf[0:POOL_HALO, :] = jnp.zeros((POOL_HALO, D), F32)

    @pl.when(s > 0)
    def _():
        ubuf[0:CONV_HALO, :] = ubuf[ts:ts + CONV_HALO, :]
        pbuf[0:POOL_HALO, :] = pbuf[ts:ts + POOL_HALO, :]

    x = x_ref[0]
    shift1 = mod_ref[0, 0:1, :]
    scale1 = mod_ref[0, 1:2, :]
    gate1 = mod_ref[0, 2:3, :]
    h = _rmsnorm_mod(x, n1g_ref[...], shift1, scale1).astype(BF16)

    def proj(j):
        cols = slice(j * D, (j + 1) * D)
        return jnp.dot(h, win_ref[:, cols], preferred_element_type=F32) + bin_ref[:, cols]

    za = proj(0)
    zb = proj(1)
    ubuf[CONV_HALO:CONV_HALO + ts, :] = za * jax.nn.sigmoid(zb)
    zp = proj(2)
    pbuf[POOL_HALO:POOL_HALO + ts, :] = zp

    def conv_block(i, carry):
        r0 = pl.multiple_of(i * CONV_ROWS, CONV_ROWS)
        for lb in range(D // CONV_LANES):
            lanes = slice(lb * CONV_LANES, (lb + 1) * CONV_LANES)
            win = ubuf[pl.ds(r0, CONV_ROWS + CONV_HALO), lanes]
            acc = jnp.broadcast_to(cb_ref[:, lanes], (CONV_ROWS, CONV_LANES))
            for r in range(SUBLANES):
                shifted = win if r == 0 else pltpu.roll(win, r, axis=0)
                for q in range(CONV_HALO // SUBLANES):
                    lag = SUBLANES * q + r
                    if lag >= CONV_KERNEL:
                        continue
                    k = CONV_KERNEL - 1 - lag
                    x0 = CONV_HALO - SUBLANES * q
                    acc = acc + cw_ref[k:k + 1, lanes] * shifted[x0:x0 + CONV_ROWS, :]
            cbuf[pl.ds(r0, CONV_ROWS), lanes] = acc
        return carry

    lax.fori_loop(0, ts // CONV_ROWS, conv_block, 0)

    u = cbuf[...]
    mu = jnp.mean(u, axis=-1, keepdims=True)
    uc = u - mu
    var = jnp.mean(uc * uc, axis=-1, keepdims=True)
    un = uc * lax.rsqrt(var + EPS) * lng_ref[...] + lnb_ref[...]
    act = (un * jax.nn.sigmoid(un)).astype(BF16)
    conv_out = jnp.dot(act, wco_ref[...], preferred_element_type=F32)

    n_groups = len(POOL_WINDOWS)
    pg = D // n_groups
    frame = lax.broadcasted_iota(jnp.int32, (ts, pg), 0) + (s * ts + 1)
    pool_parts = []
    for g, w in enumerate(POOL_WINDOWS):
        cols = slice(g * pg, (g + 1) * pg)
        tot = pbuf[POOL_HALO:POOL_HALO + ts, cols]
        for d in range(1, w):
            tot = tot + pbuf[POOL_HALO - d:POOL_HALO - d + ts, cols]
        cnt = jnp.minimum(frame, w).astype(F32)
        pooled = tot / cnt - zp[:, cols]
        pool_parts.append(jnp.dot(pooled.astype(BF16), pw_ref[g], preferred_element_type=F32))
    pool_out = jnp.concatenate(pool_parts, axis=-1) * ps_ref[...]

    ga = jax.nn.sigmoid(proj(3))
    gb = jax.nn.sigmoid(proj(4))
    y = (ga * conv_out + gb * pool_out).astype(BF16)
    o_ref[0] = x + gate1 * (jnp.dot(y, wout_ref[...], preferred_element_type=F32) + bout_ref[...])


def _mixer_call(x, mod_l, n1g, w_in, b_in, conv_w, conv_b, ln_g, ln_b, w_co, pool_w, pool_scale, w_out, b_out):
    B, S, D = x.shape
    ts = MIXER_ROWS
    d_in = w_in.shape[1]
    n_groups, pg, _ = pool_w.shape
    const2 = lambda b, s: (0, 0)
    const3 = lambda b, s: (0, 0, 0)
    row = lambda v: v.reshape(1, -1)
    return pl.pallas_call(
        _mixer_kernel,
        grid=(B, S // ts),
        in_specs=[
            pl.BlockSpec((1, ts, D), lambda b, s: (b, s, 0)),
            pl.BlockSpec((1, N_MOD, D), lambda b, s: (b, 0, 0)),
            pl.BlockSpec((1, D), const2),
            pl.BlockSpec((D, d_in), const2),
            pl.BlockSpec((1, d_in), const2),
            pl.BlockSpec((CONV_KERNEL, D), const2),
            pl.BlockSpec((1, D), const2),
            pl.BlockSpec((1, D), const2),
            pl.BlockSpec((1, D), const2),
            pl.BlockSpec((D, D), const2),
            pl.BlockSpec((n_groups, pg, pg), const3),
            pl.BlockSpec((1, D), const2),
            pl.BlockSpec((D, D), const2),
            pl.BlockSpec((1, D), const2),
        ],
        out_specs=pl.BlockSpec((1, ts, D), lambda b, s: (b, s, 0)),
        out_shape=jax.ShapeDtypeStruct((B, S, D), F32),
        scratch_shapes=[
            pltpu.VMEM((CONV_HALO + ts, D), F32),
            pltpu.VMEM((POOL_HALO + ts, D), F32),
            pltpu.VMEM((ts, D), F32),
        ],
        compiler_params=pltpu.CompilerParams(
            dimension_semantics=("arbitrary", "arbitrary"), vmem_limit_bytes=VMEM_LIMIT_BYTES),
        name="mixer",
    )(x, mod_l, row(n1g), w_in.astype(BF16), row(b_in), conv_w, row(conv_b), row(ln_g), row(ln_b),
      w_co.astype(BF16), pool_w.astype(BF16), row(pool_scale), w_out.astype(BF16), row(b_out))


def _bf16_pair_bits(x):
    hi = pltpu.bitcast(x.astype(BF16).astype(F32), jnp.uint32)
    return hi | (hi >> 16)


def _row_as_packed_bf16(ref, hd, row, lanes):
    words = jnp.broadcast_to(ref[hd, row:row + 1, lanes], (SUBLANES, LANES))
    return pltpu.bitcast(words, BF16)


def _top_ranks(s):
    rank = jnp.full(s.shape, float(PEER_TOPK), F32)
    cur = s
    tops = []
    for r in range(PEER_TOPK):
        m = jnp.max(cur, axis=0, keepdims=True)
        hit = cur == m
        rank = jnp.where(hit, float(r), rank)
        cur = jnp.where(hit, NEG_INF, cur)
        tops.append(m)
    return rank, tops


def _route_kernel(x_ref, mod_ref, n2g_ref, wqT_ref, keys_ref, hT_ref, e1_ref, cnt_ref, r2_ref, e2_ref,
                  qT_scr, s_scr):
    tc = x_ref.shape[0]
    shift2 = mod_ref[0, 3:4, :]
    scale2 = mod_ref[0, 4:5, :]
    h = _rmsnorm_mod(x_ref[...], n2g_ref[...], shift2, scale2)
    hT = h.T.astype(BF16)
    hT_ref[...] = hT
    qT_scr[...] = jnp.dot(wqT_ref[...], hT, preferred_element_type=F32)

    row16 = lax.broadcasted_iota(jnp.int32, (PEER_TOPK, LANES), 0)
    row8 = lax.broadcasted_iota(jnp.int32, (SUBLANES, LANES), 0)

    def head_body(hd, carry):
        for p in range(2):
            k0 = pl.multiple_of(hd * (2 * PEER_NKEYS) + p * PEER_NKEYS, PEER_NKEYS)
            q_hp = qT_scr[pl.ds(k0, PEER_NKEYS), :]
            s_scr[p] = jnp.dot(keys_ref[hd, p], q_hp, preferred_element_type=F32,
                               precision=lax.Precision.HIGHEST)

        def chunk_pair_body(c2, carry2):
            for sub in range(ROUTE_CHUNKS_PER_ITER):
                route_chunk(c2 * ROUTE_CHUNKS_PER_ITER + sub)
            return carry2

        def route_chunk(c):
            l0 = pl.multiple_of(c * LANES, LANES)
            s1 = s_scr[0, :, pl.ds(l0, LANES)]
            s2 = s_scr[1, :, pl.ds(l0, LANES)]
            rank1, top1 = _top_ranks(s1)
            rank2, top2 = _top_ranks(s2)
            top1a = jnp.zeros((PEER_TOPK, LANES), F32)
            top2a = jnp.zeros((PEER_TOPK, LANES), F32)
            for r in range(PEER_TOPK):
                top1a = jnp.where(row16 == r, top1[r], top1a)
                top2a = jnp.where(row16 == r, top2[r], top2a)

            lo2, hi2 = top2a[0:SUBLANES], top2a[SUBLANES:]
            lo1, hi1 = top1a[0:SUBLANES], top1a[SUBLANES:]
            by_j = [top1[0] + lo2, top1[0] + hi2, top1[1] + lo2]
            by_j += [jnp.where(row8 < PEER_TOPK // (i + 1), top1[i] + lo2, NEG_INF) for i in (2, 3, 4)]
            by_i = [jnp.where(row8 >= 5, lo1 + top2[j], NEG_INF) for j in (0, 1)]
            by_i8 = hi1 + top2[0]
            slabs = by_j + by_i + [by_i8]
            cur = list(slabs)
            thr = None
            for r in range(PEER_TOPK):
                m8 = cur[0]
                for sl in cur[1:]:
                    m8 = jnp.maximum(m8, sl)
                thr = jnp.max(m8, axis=0, keepdims=True)
                if r + 1 < PEER_TOPK:
                    cur = [jnp.where(sl == thr, NEG_INF, sl) for sl in cur]

            cmax = top1[0] + top2[0]
            ones = [jnp.where(slab >= thr, 1.0, 0.0) for slab in slabs]
            zsum = jnp.zeros((1, LANES), F32)
            for slab, one in zip(slabs, ones):
                zsum = zsum + jnp.sum(one * jnp.exp(slab - cmax), axis=0, keepdims=True)
            col_sum = lambda v: jnp.sum(v, axis=0, keepdims=True)
            per_i = [col_sum(ones[0]) + col_sum(ones[1])] + [col_sum(ones[k]) for k in (2, 3, 4, 5)]
            tail_lo = ones[6] + ones[7]
            per_i += [tail_lo[i:i + 1] for i in (5, 6, 7)]
            per_i += [ones[8][i:i + 1] for i in range(SUBLANES)]
            cnt = jnp.zeros(s1.shape, F32)
            for i, n_i in enumerate(per_i):
                cnt = jnp.where(rank1 == float(i), n_i, cnt)

            e1_ref[hd, :, pl.ds(l0, LANES)] = _bf16_pair_bits(jnp.exp(s1 - top1[0]))
            cnt_ref[hd, :, pl.ds(l0, LANES)] = _bf16_pair_bits(cnt)
            r2_ref[hd, :, pl.ds(l0, LANES)] = pltpu.bitcast(rank2.astype(BF16), jnp.uint32)
            e2_ref[hd, :, pl.ds(l0, LANES)] = pltpu.bitcast((jnp.exp(s2 - top2[0]) / zsum).astype(BF16), jnp.uint32)

        lax.fori_loop(0, tc // (LANES * ROUTE_CHUNKS_PER_ITER), chunk_pair_body, 0)
        return carry

    lax.fori_loop(0, PEER_HEADS, head_body, 0)


def _route_call(x2d, mod_l, n2g, wqT, keys, seq_len):
    T, D = x2d.shape
    tc = ROUTE_COLS
    HK = wqT.shape[0]
    steps_per_seq = seq_len // tc
    route_shape = jax.ShapeDtypeStruct((PEER_HEADS, PEER_NKEYS, T), jnp.uint32)
    route_bf = jax.ShapeDtypeStruct((PEER_HEADS, PEER_NKEYS // 2, T), jnp.uint32)
    route_spec = pl.BlockSpec((PEER_HEADS, PEER_NKEYS, tc), lambda i: (0, 0, i))
    pair_spec = pl.BlockSpec((PEER_HEADS, PEER_NKEYS // 2, tc), lambda i: (0, 0, i))
    return pl.pallas_call(
        _route_kernel,
        grid=(T // tc,),
        in_specs=[
            pl.BlockSpec((tc, D), lambda i: (i, 0)),
            pl.BlockSpec((1, N_MOD, D), lambda i: (i // steps_per_seq, 0, 0)),
            pl.BlockSpec((1, D), lambda i: (0, 0)),
            pl.BlockSpec((HK, D), lambda i: (0, 0)),
            pl.BlockSpec(keys.shape, lambda i: (0, 0, 0, 0)),
        ],
        out_specs=[pl.BlockSpec((D, tc), lambda i: (0, i)), route_spec, route_spec, pair_spec, pair_spec],
        out_shape=[jax.ShapeDtypeStruct((D, T), BF16), route_shape, route_shape, route_bf, route_bf],
        scratch_shapes=[
            pltpu.VMEM((HK, tc), F32),
            pltpu.VMEM((2, PEER_NKEYS, tc), F32),
        ],
        compiler_params=pltpu.CompilerParams(
            dimension_semantics=("arbitrary",), vmem_limit_bytes=VMEM_LIMIT_BYTES),
        name="peer_route",
    )(x2d, mod_l, n2g.reshape(1, D), wqT, keys)


def _gelu_exact(x):
    return 0.5 * x * (1.0 + lax.erf(x * (1.0 / math.sqrt(2.0))))


def _peer_tile(n, lag, n_tiles, n_expert_tiles):
    t = jnp.clip(n - lag, 0, n_tiles - 1)
    return t // n_expert_tiles, t % n_expert_tiles


def _peer_kernel(hT_ref, e1_ref, cnt_ref, r2_ref, e2_ref, u_ref, vT_ref, x_ref, mod_ref, fg_ref, o_ref,
                 acc, p_even, p_odd, *, final_norm, n_tiles, n_expert_tiles):
    te = vT_ref.shape[1]
    tc = hT_ref.shape[1]
    n = pl.program_id(0)
    _, j_val = _peer_tile(n, 1, n_tiles, n_expert_tiles)

    @pl.when(n == 0)
    def _():
        for ref in (p_even, p_odd):
            ref[...] = jnp.zeros(ref.shape, ref.dtype)

    @pl.when(j_val == 0)
    def _():
        acc[...] = jnp.zeros(acc.shape, F32)

    a_per_piece = PEER_PIECE // PEER_NKEYS
    n_pieces = te // PEER_PIECE
    n_chunks = tc // LANES
    groups = PEER_NKEYS // BF16_ROWS
    zero_bf = jnp.zeros((BF16_ROWS, LANES), BF16)

    def step(p_write, p_read):
        acc[...] += jnp.dot(pltpu.bitcast(vT_ref[...], BF16), p_read[...], preferred_element_type=F32)
        for q in range(n_pieces):
            words_q = slice(q * PEER_PIECE // 2, (q + 1) * PEER_PIECE // 2)
            pre = jnp.dot(pltpu.bitcast(u_ref[words_q, :], BF16), hT_ref[...],
                          preferred_element_type=F32)
            for c in range(n_chunks):
                lanes = slice(c * LANES, (c + 1) * LANES)
                ws = [[None] * groups for _ in range(a_per_piece)]
                for hd in range(PEER_HEADS):
                    cnt_rows = [_row_as_packed_bf16(cnt_ref, hd, q * a_per_piece + al, lanes)
                                for al in range(a_per_piece)]
                    e1_rows = [_row_as_packed_bf16(e1_ref, hd, q * a_per_piece + al, lanes)
                               for al in range(a_per_piece)]
                    for g in range(groups):
                        words = slice(g * SUBLANES, (g + 1) * SUBLANES)
                        r2 = pltpu.bitcast(r2_ref[hd, words, lanes], BF16)
                        e2 = pltpu.bitcast(e2_ref[hd, words, lanes], BF16)
                        for al in range(a_per_piece):
                            term = jnp.where(r2 < cnt_rows[al], e2, zero_bf) * e1_rows[al]
                            ws[al][g] = term if hd == 0 else ws[al][g] + term
                for al in range(a_per_piece):
                    for g in range(groups):
                        rl = al * PEER_NKEYS + g * BF16_ROWS
                        act = _gelu_exact(pre[rl:rl + BF16_ROWS, lanes]).astype(BF16)
                        p_write[q * PEER_PIECE + rl:q * PEER_PIECE + rl + BF16_ROWS, lanes] = ws[al][g] * act

    @pl.when(n % 2 == 0)
    def _():
        step(p_even, p_odd)

    @pl.when(n % 2 == 1)
    def _():
        step(p_odd, p_even)

    @pl.when(jnp.logical_and(n >= 1, j_val == n_expert_tiles - 1))
    def _():
        gate2 = mod_ref[0, 5:6, :]
        out = x_ref[...] + gate2 * acc[...].T
        if final_norm:
            out = out * lax.rsqrt(jnp.mean(out * out, axis=-1, keepdims=True) + EPS) * fg_ref[...]
        o_ref[...] = out


def _peer_call(hT, e1, cnt, r2, e2, u_words, vT_words, x2d, mod_l, final_g, seq_len, final_norm):
    T, D = x2d.shape
    E = vT_words.shape[1]
    tc = PEER_COLS
    te = PEER_EXPERT_TILE
    steps_per_seq = seq_len // tc
    n_expert_tiles = E // te
    n_tiles = (T // tc) * n_expert_tiles
    tile = functools.partial(_peer_tile, n_tiles=n_tiles, n_expert_tiles=n_expert_tiles)
    packed_spec = pl.BlockSpec((PEER_HEADS, PEER_NKEYS // 2, tc), lambda n: (0, 0, tile(n, 0)[0]))
    key_spec = pl.BlockSpec((PEER_HEADS, te // PEER_NKEYS, tc), lambda n: (0, tile(n, 0)[1], tile(n, 0)[0]))
    return pl.pallas_call(
        functools.partial(_peer_kernel, final_norm=final_norm, n_tiles=n_tiles, n_expert_tiles=n_expert_tiles),
        grid=(n_tiles + 1,),
        in_specs=[
            pl.BlockSpec((D, tc), lambda n: (0, tile(n, 0)[0])),
            key_spec, key_spec, packed_spec, packed_spec,
            pl.BlockSpec((te // 2, D), lambda n: (tile(n, 0)[1], 0)),
            pl.BlockSpec((D // 2, te), lambda n: (0, tile(n, 1)[1])),
            pl.BlockSpec((tc, D), lambda n: (tile(n, 1)[0], 0)),
            pl.BlockSpec((1, N_MOD, D), lambda n: (tile(n, 1)[0] // steps_per_seq, 0, 0)),
            pl.BlockSpec((1, D), lambda n: (0, 0)),
        ],
        out_specs=pl.BlockSpec((tc, D), lambda n: (tile(n, 1)[0], 0)),
        out_shape=jax.ShapeDtypeStruct((T, D), F32),
        scratch_shapes=[
            pltpu.VMEM((D, tc), F32),
            pltpu.VMEM((te, tc), BF16),
            pltpu.VMEM((te, tc), BF16),
        ],
        compiler_params=pltpu.CompilerParams(
            dimension_semantics=("arbitrary",), vmem_limit_bytes=VMEM_LIMIT_BYTES),
        name="peer_experts",
    )(hT, e1, cnt, r2, e2, u_words, vT_words, x2d, mod_l, final_g.reshape(1, D))


def _pack_kernel(w_ref, o_ref, *, transpose):
    w = w_ref[...]
    if transpose:
        w = w.T
    o_ref[...] = pltpu.bitcast(w.astype(BF16), jnp.uint32)


def _pack_row_pairs(w, transpose):
    R, C = w.shape
    blk = PACK_ROWS
    if transpose:
        out_shape, out_spec = (C // 2, R), pl.BlockSpec((C // 2, blk), lambda i: (0, i))
    else:
        out_shape, out_spec = (R // 2, C), pl.BlockSpec((blk // 2, C), lambda i: (i, 0))
    return pl.pallas_call(
        functools.partial(_pack_kernel, transpose=transpose),
        grid=(R // blk,),
        in_specs=[pl.BlockSpec((blk, C), lambda i: (i, 0))],
        out_specs=out_spec,
        out_shape=jax.ShapeDtypeStruct(out_shape, jnp.uint32),
        compiler_params=pltpu.CompilerParams(
            dimension_semantics=("arbitrary",), vmem_limit_bytes=VMEM_LIMIT_BYTES),
        name="pack_table",
    )(w)


@jax.jit
def _forward(x, c, ada_w, ada_b, norm1_g, w_in, b_in, conv_w, conv_b, conv_ln_g, conv_ln_b, w_conv_out,
             pool_w, pool_scale, w_out, b_out, norm2_g, peer_wq, peer_keys, peer_u, peer_v, final_g):
    B, S, D = x.shape
    L = ada_w.shape[0]
    c_pad = jnp.zeros((SUBLANES, D), F32).at[:B].set(c)
    mod = _ada_call(c_pad, ada_w, ada_b)[:, :B].reshape(L, B, N_MOD, D)
    for l in range(L):
        x = _mixer_call(x, mod[l], norm1_g[l], w_in[l], b_in[l], conv_w[l], conv_b[l], conv_ln_g[l],
                        conv_ln_b[l], w_conv_out[l], pool_w[l], pool_scale[l], w_out[l], b_out[l])
        x2d = x.reshape(B * S, D)
        hT, e1, cnt, r2, e2 = _route_call(x2d, mod[l], norm2_g[l], peer_wq[l].T.astype(BF16), peer_keys[l], S)
        x2d = _peer_call(hT, e1, cnt, r2, e2, _pack_row_pairs(peer_u[l], transpose=False),
                         _pack_row_pairs(peer_v[l], transpose=True), x2d, mod[l],
                         final_g, S, final_norm=(l == L - 1))
        x = x2d.reshape(B, S, D)
    return x


def kernel(x, c, ada_w, ada_b, norm1_g, w_in, b_in, conv_w, conv_b, conv_ln_g, conv_ln_b, w_conv_out, pool_w,
           pool_scale, w_out, b_out, norm2_g, peer_wq, peer_keys, peer_u, peer_v, final_g):
    return _forward(x, c, ada_w, ada_b, norm1_g, w_in, b_in, conv_w, conv_b, conv_ln_g, conv_ln_b, w_conv_out,
                    pool_w, pool_scale, w_out, b_out, norm2_g, peer_wq, peer_keys, peer_u, peer_v, final_g)
```

```python
import functools
import math

import jax
import jax.numpy as jnp
from jax import lax
from jax.experimental import pallas as pl
from jax.experimental.pallas import tpu as pltpu

F32 = jnp.float32
BF16 = jnp.bfloat16

EPS = 1e-6
CONV_KERNEL = 31
POOL_WINDOWS = (2, 4, 8, 16)
N_MOD = 6
PEER_HEADS = 8
PEER_NKEYS = 128
PEER_TOPK = 16

LANES = 128
SUBLANES = 8
BF16_ROWS = 16
VMEM_LIMIT_BYTES = 56 * 1024 * 1024

MIXER_ROWS = 512
CONV_HALO = 32
POOL_HALO = 16
CONV_ROWS = 64
CONV_LANES = 128
ROUTE_COLS = 512
ROUTE_CHUNKS_PER_ITER = 4
PEER_COLS = 512
PEER_EXPERT_TILE = 1024
PEER_PIECE = 256
PACK_ROWS = 1024

NEG_INF = float("-inf")


def _rmsnorm_mod(x, g, shift, scale):
    y = x * lax.rsqrt(jnp.mean(x * x, axis=-1, keepdims=True) + EPS) * g
    return y * (1.0 + scale) + shift


def _ada_kernel(c_ref, w_ref, b_ref, o_ref):
    c = c_ref[...]
    act = c * jax.nn.sigmoid(c)
    o_ref[0] = jnp.dot(act, w_ref[0], preferred_element_type=F32,
                       precision=lax.Precision.HIGHEST) + b_ref[0]


def _ada_call(c_pad, ada_w, ada_b):
    L, D, ND = ada_w.shape
    rows = c_pad.shape[0]
    return pl.pallas_call(
        _ada_kernel,
        grid=(L, ND // D),
        in_specs=[
            pl.BlockSpec((rows, D), lambda l, n: (0, 0)),
            pl.BlockSpec((1, D, D), lambda l, n: (l, 0, n)),
            pl.BlockSpec((1, 1, D), lambda l, n: (l, 0, n)),
        ],
        out_specs=pl.BlockSpec((1, rows, D), lambda l, n: (l, 0, n)),
        out_shape=jax.ShapeDtypeStruct((L, rows, ND), F32),
        compiler_params=pltpu.CompilerParams(
            dimension_semantics=("arbitrary", "arbitrary"), vmem_limit_bytes=VMEM_LIMIT_BYTES),
        name="ada_mod",
    )(c_pad, ada_w, ada_b.reshape(L, 1, ND))


def _mixer_kernel(x_ref, mod_ref, n1g_ref, win_ref, bin_ref, cw_ref, cb_ref, lng_ref, lnb_ref,
                  wco_ref, pw_ref, ps_ref, wout_ref, bout_ref, o_ref, ubuf, pbuf, cbuf):
    ts, D = cbuf.shape
    s = pl.program_id(1)

    @pl.when(s == 0)
    def _():
        ubuf[0:CONV_HALO, :] = jnp.zeros((CONV_HALO, D), F32)
        pbuf[0:POOL_HALO, :] = jnp.zeros((POOL_HALO, D), F32)

    @pl.when(s > 0)
    def _():
        ubuf[0:CONV_HALO, :] = ubuf[ts:ts + CONV_HALO, :]
        pbuf[0:POOL_HALO, :] = pbuf[ts:ts + POOL_HALO, :]

    x = x_ref[0]
    shift1 = mod_ref[0, 0:1, :]
    scale1 = mod_ref[0, 1:2, :]
    gate1 = mod_ref[0, 2:3, :]
    h = _rmsnorm_mod(x, n1g_ref[...], shift1, scale1).astype(BF16)

    def proj(j):
        cols = slice(j * D, (j + 1) * D)
        return jnp.dot(h, win_ref[:, cols], preferred_element_type=F32) + bin_ref[:, cols]

    za = proj(0)
    zb = proj(1)
    ubuf[CONV_HALO:CONV_HALO + ts, :] = za * jax.nn.sigmoid(zb)
    zp = proj(2)
    pbuf[POOL_HALO:POOL_HALO + ts, :] = zp

    def conv_block(i, carry):
        r0 = pl.multiple_of(i * CONV_ROWS, CONV_ROWS)
        for lb in range(D // CONV_LANES):
            lanes = slice(lb * CONV_LANES, (lb + 1) * CONV_LANES)
            win = ubuf[pl.ds(r0, CONV_ROWS + CONV_HALO), lanes]
            acc = jnp.broadcast_to(cb_ref[:, lanes], (CONV_ROWS, CONV_LANES))
            for r in range(SUBLANES):
                shifted = win if r == 0 else pltpu.roll(win, r, axis=0)
                for q in range(CONV_HALO // SUBLANES):
                    lag = SUBLANES * q + r
                    if lag >= CONV_KERNEL:
                        continue
                    k = CONV_KERNEL - 1 - lag
                    x0 = CONV_HALO - SUBLANES * q
                    acc = acc + cw_ref[k:k + 1, lanes] * shifted[x0:x0 + CONV_ROWS, :]
            cbuf[pl.ds(r0, CONV_ROWS), lanes] = acc
        return carry

    lax.fori_loop(0, ts // CONV_ROWS, conv_block, 0)

    u = cbuf[...]
    mu = jnp.mean(u, axis=-1, keepdims=True)
    uc = u - mu
    var = jnp.mean(uc * uc, axis=-1, keepdims=True)
    un = uc * lax.rsqrt(var + EPS) * lng_ref[...] + lnb_ref[...]
    act = (un * jax.nn.sigmoid(un)).astype(BF16)
    conv_out = jnp.dot(act, wco_ref[...], preferred_element_type=F32)

    n_groups = len(POOL_WINDOWS)
    pg = D // n_groups
    frame = lax.broadcasted_iota(jnp.int32, (ts, pg), 0) + (s * ts + 1)
    pool_parts = []
    for g, w in enumerate(POOL_WINDOWS):
        cols = slice(g * pg, (g + 1) * pg)
        tot = pbuf[POOL_HALO:POOL_HALO + ts, cols]
        for d in range(1, w):
            tot = tot + pbuf[POOL_HALO - d:POOL_HALO - d + ts, cols]
        cnt = jnp.minimum(frame, w).astype(F32)
        pooled = tot / cnt - zp[:, cols]
        pool_parts.append(jnp.dot(pooled.astype(BF16), pw_ref[g], preferred_element_type=F32))
    pool_out = jnp.concatenate(pool_parts, axis=-1) * ps_ref[...]

    ga = jax.nn.sigmoid(proj(3))
    gb = jax.nn.sigmoid(proj(4))
    y = (ga * conv_out + gb * pool_out).astype(BF16)
    o_ref[0] = x + gate1 * (jnp.dot(y, wout_ref[...], preferred_element_type=F32) + bout_ref[...])


def _mixer_call(x, mod_l, n1g, w_in, b_in, conv_w, conv_b, ln_g, ln_b, w_co, pool_w, pool_scale, w_out, b_out):
    B, S, D = x.shape
    ts = MIXER_ROWS
    d_in = w_in.shape[1]
    n_groups, pg, _ = pool_w.shape
    const2 = lambda b, s: (0, 0)
    const3 = lambda b, s: (0, 0, 0)
    row = lambda v: v.reshape(1, -1)
    return pl.pallas_call(
        _mixer_kernel,
        grid=(B, S // ts),
        in_specs=[
            pl.BlockSpec((1, ts, D), lambda b, s: (b, s, 0)),
            pl.BlockSpec((1, N_MOD, D), lambda b, s: (b, 0, 0)),
            pl.BlockSpec((1, D), const2),
            pl.BlockSpec((D, d_in), const2),
            pl.BlockSpec((1, d_in), const2),
            pl.BlockSpec((CONV_KERNEL, D), const2),
            pl.BlockSpec((1, D), const2),
            pl.BlockSpec((1, D), const2),
            pl.BlockSpec((1, D), const2),
            pl.BlockSpec((D, D), const2),
            pl.BlockSpec((n_groups, pg, pg), const3),
            pl.BlockSpec((1, D), const2),
            pl.BlockSpec((D, D), const2),
            pl.BlockSpec((1, D), const2),
        ],
        out_specs=pl.BlockSpec((1, ts, D), lambda b, s: (b, s, 0)),
        out_shape=jax.ShapeDtypeStruct((B, S, D), F32),
        scratch_shapes=[
            pltpu.VMEM((CONV_HALO + ts, D), F32),
            pltpu.VMEM((POOL_HALO + ts, D), F32),
            pltpu.VMEM((ts, D), F32),
        ],
        compiler_params=pltpu.CompilerParams(
            dimension_semantics=("arbitrary", "arbitrary"), vmem_limit_bytes=VMEM_LIMIT_BYTES),
        name="mixer",
    )(x, mod_l, row(n1g), w_in.astype(BF16), row(b_in), conv_w, row(conv_b), row(ln_g), row(ln_b),
      w_co.astype(BF16), pool_w.astype(BF16), row(pool_scale), w_out.astype(BF16), row(b_out))


def _bf16_pair_bits(x):
    hi = pltpu.bitcast(x.astype(BF16).astype(F32), jnp.uint32)
    return hi | (hi >> 16)


def _row_as_packed_bf16(ref, hd, row, lanes):
    words = jnp.broadcast_to(ref[hd, row:row + 1, lanes], (SUBLANES, LANES))
    return pltpu.bitcast(words, BF16)


def _top_ranks(s):
    rank = jnp.full(s.shape, float(PEER_TOPK), F32)
    cur = s
    tops = []
    for r in range(PEER_TOPK):
        m = jnp.max(cur, axis=0, keepdims=True)
        hit = cur == m
        rank = jnp.where(hit, float(r), rank)
        cur = jnp.where(hit, NEG_INF, cur)
        tops.append(m)
    return rank, tops


def _route_kernel(x_ref, mod_ref, n2g_ref, wqT_ref, keys_ref, hT_ref, e1_ref, cnt_ref, r2_ref, e2_ref,
                  qT_scr, s_scr):
    tc = x_ref.shape[0]
    shift2 = mod_ref[0, 3:4, :]
    scale2 = mod_ref[0, 4:5, :]
    h = _rmsnorm_mod(x_ref[...], n2g_ref[...], shift2, scale2)
    hT = h.T.astype(BF16)
    hT_ref[...] = hT
    qT_scr[...] = jnp.dot(wqT_ref[...], hT, preferred_element_type=F32)

    row16 = lax.broadcasted_iota(jnp.int32, (PEER_TOPK, LANES), 0)
    row8 = lax.broadcasted_iota(jnp.int32, (SUBLANES, LANES), 0)

    def head_body(hd, carry):
        for p in range(2):
            k0 = pl.multiple_of(hd * (2 * PEER_NKEYS) + p * PEER_NKEYS, PEER_NKEYS)
            q_hp = qT_scr[pl.ds(k0, PEER_NKEYS), :]
            s_scr[p] = jnp.dot(keys_ref[hd, p], q_hp, preferred_element_type=F32,
                               precision=lax.Precision.HIGHEST)

        def chunk_pair_body(c2, carry2):
            for sub in range(ROUTE_CHUNKS_PER_ITER):
                route_chunk(c2 * ROUTE_CHUNKS_PER_ITER + sub)
            return carry2

        def route_chunk(c):
            l0 = pl.multiple_of(c * LANES, LANES)
            s1 = s_scr[0, :, pl.ds(l0, LANES)]
            s2 = s_scr[1, :, pl.ds(l0, LANES)]
            rank1, top1 = _top_ranks(s1)
            rank2, top2 = _top_ranks(s2)
            top1a = jnp.zeros((PEER_TOPK, LANES), F32)
            top2a = jnp.zeros((PEER_TOPK, LANES), F32)
            for r in range(PEER_TOPK):
                top1a = jnp.where(row16 == r, top1[r], top1a)
                top2a = jnp.where(row16 == r, top2[r], top2a)

            lo2, hi2 = top2a[0:SUBLANES], top2a[SUBLANES:]
            lo1, hi1 = top1a[0:SUBLANES], top1a[SUBLANES:]
            by_j = [top1[0] + lo2, top1[0] + hi2, top1[1] + lo2]
            by_j += [jnp.where(row8 < PEER_TOPK // (i + 1), top1[i] + lo2, NEG_INF) for i in (2, 3, 4)]
            by_i = [jnp.where(row8 >= 5, lo1 + top2[j], NEG_INF) for j in (0, 1)]
            by_i8 = hi1 + top2[0]
            slabs = by_j + by_i + [by_i8]
            cur = list(slabs)
            thr = None
            for r in range(PEER_TOPK):
                m8 = cur[0]
                for sl in cur[1:]:
                    m8 = jnp.maximum(m8, sl)
                thr = jnp.max(m8, axis=0, keepdims=True)
                if r + 1 < PEER_TOPK:
                    cur = [jnp.where(sl == thr, NEG_INF, sl) for sl in cur]

            cmax = top1[0] + top2[0]
            ones = [jnp.where(slab >= thr, 1.0, 0.0) for slab in slabs]
            zsum = jnp.zeros((1, LANES), F32)
            for slab, one in zip(slabs, ones):
                zsum = zsum + jnp.sum(one * jnp.exp(slab - cmax), axis=0, keepdims=True)
            col_sum = lambda v: jnp.sum(v, axis=0, keepdims=True)
            per_i = [col_sum(ones[0]) + col_sum(ones[1])] + [col_sum(ones[k]) for k in (2, 3, 4, 5)]
            tail_lo = ones[6] + ones[7]
            per_i += [tail_lo[i:i + 1] for i in (5, 6, 7)]
            per_i += [ones[8][i:i + 1] for i in range(SUBLANES)]
            cnt = jnp.zeros(s1.shape, F32)
            for i, n_i in enumerate(per_i):
                cnt = jnp.where(rank1 == float(i), n_i, cnt)

            e1_ref[hd, :, pl.ds(l0, LANES)] = _bf16_pair_bits(jnp.exp(s1 - top1[0]))
            cnt_ref[hd, :, pl.ds(l0, LANES)] = _bf16_pair_bits(cnt)
            r2_ref[hd, :, pl.ds(l0, LANES)] = pltpu.bitcast(rank2.astype(BF16), jnp.uint32)
            e2_ref[hd, :, pl.ds(l0, LANES)] = pltpu.bitcast((jnp.exp(s2 - top2[0]) / zsum).astype(BF16), jnp.uint32)

        lax.fori_loop(0, tc // (LANES * ROUTE_CHUNKS_PER_ITER), chunk_pair_body, 0)
        return carry

    lax.fori_loop(0, PEER_HEADS, head_body, 0)


def _route_call(x2d, mod_l, n2g, wqT, keys, seq_len):
    T, D = x2d.shape
    tc = ROUTE_COLS
    HK = wqT.shape[0]
    steps_per_seq = seq_len // tc
    route_shape = jax.ShapeDtypeStruct((PEER_HEADS, PEER_NKEYS, T), jnp.uint32)
    route_bf = jax.ShapeDtypeStruct((PEER_HEADS, PEER_NKEYS // 2, T), jnp.uint32)
    route_spec = pl.BlockSpec((PEER_HEADS, PEER_NKEYS, tc), lambda i: (0, 0, i))
    pair_spec = pl.BlockSpec((PEER_HEADS, PEER_NKEYS // 2, tc), lambda i: (0, 0, i))
    return pl.pallas_call(
        _route_kernel,
        grid=(T // tc,),
        in_specs=[
            pl.BlockSpec((tc, D), lambda i: (i, 0)),
            pl.BlockSpec((1, N_MOD, D), lambda i: (i // steps_per_seq, 0, 0)),
            pl.BlockSpec((1, D), lambda i: (0, 0)),
            pl.BlockSpec((HK, D), lambda i: (0, 0)),
            pl.BlockSpec(keys.shape, lambda i: (0, 0, 0, 0)),
        ],
        out_specs=[pl.BlockSpec((D, tc), lambda i: (0, i)), route_spec, route_spec, pair_spec, pair_spec],
        out_shape=[jax.ShapeDtypeStruct((D, T), BF16), route_shape, route_shape, route_bf, route_bf],
        scratch_shapes=[
            pltpu.VMEM((HK, tc), F32),
            pltpu.VMEM((2, PEER_NKEYS, tc), F32),
        ],
        compiler_params=pltpu.CompilerParams(
            dimension_semantics=("arbitrary",), vmem_limit_bytes=VMEM_LIMIT_BYTES),
        name="peer_route",
    )(x2d, mod_l, n2g.reshape(1, D), wqT, keys)


def _gelu_exact(x):
    return 0.5 * x * (1.0 + lax.erf(x * (1.0 / math.sqrt(2.0))))


def _peer_tile(n, lag, n_tiles, n_expert_tiles):
    t = jnp.clip(n - lag, 0, n_tiles - 1)
    return t // n_expert_tiles, t % n_expert_tiles


def _peer_kernel(hT_ref, e1_ref, cnt_ref, r2_ref, e2_ref, u_ref, vT_ref, x_ref, mod_ref, fg_ref, o_ref,
                 acc, p_even, p_odd, *, final_norm, n_tiles, n_expert_tiles):
    te = vT_ref.shape[1]
    tc = hT_ref.shape[1]
    n = pl.program_id(0)
    _, j_val = _peer_tile(n, 1, n_tiles, n_expert_tiles)

    @pl.when(n == 0)
    def _():
        for ref in (p_even, p_odd):
            ref[...] = jnp.zeros(ref.shape, ref.dtype)

    @pl.when(j_val == 0)
    def _():
        acc[...] = jnp.zeros(acc.shape, F32)

    a_per_piece = PEER_PIECE // PEER_NKEYS
    n_pieces = te // PEER_PIECE
    n_chunks = tc // LANES
    groups = PEER_NKEYS // BF16_ROWS
    zero_bf = jnp.zeros((BF16_ROWS, LANES), BF16)

    def step(p_write, p_read):
        def scores(q):
            words_q = slice(q * PEER_PIECE // 2, (q + 1) * PEER_PIECE // 2)
            return jnp.dot(pltpu.bitcast(u_ref[words_q, :], BF16), hT_ref[...],
                           preferred_element_type=F32)

        def values(q):
            d_words = acc.shape[0] // (2 * n_pieces)
            words = slice(q * d_words, (q + 1) * d_words)
            rows = slice(2 * q * d_words, 2 * (q + 1) * d_words)
            acc[rows, :] += jnp.dot(pltpu.bitcast(vT_ref[words, :], BF16), p_read[...],
                                    preferred_element_type=F32)

        pre_next = scores(0)
        for q in range(n_pieces):
            pre = pre_next
            values(q)
            if q + 1 < n_pieces:
                pre_next = scores(q + 1)
            for c in range(n_chunks):
                lanes = slice(c * LANES, (c + 1) * LANES)
                ws = [[None] * groups for _ in range(a_per_piece)]
                for hd in range(PEER_HEADS):
                    for al in range(a_per_piece):
                        cnt_row = _row_as_packed_bf16(cnt_ref, hd, q * a_per_piece + al, lanes)
                        e1_row = _row_as_packed_bf16(e1_ref, hd, q * a_per_piece + al, lanes)
                        for g in range(groups):
                            words = slice(g * SUBLANES, (g + 1) * SUBLANES)
                            r2 = pltpu.bitcast(r2_ref[hd, words, lanes], BF16)
                            e2 = pltpu.bitcast(e2_ref[hd, words, lanes], BF16)
                            term = jnp.where(r2 < cnt_row, e2, zero_bf) * e1_row
                            ws[al][g] = term if hd == 0 else ws[al][g] + term
                for al in range(a_per_piece):
                    for g in range(groups):
                        rl = al * PEER_NKEYS + g * BF16_ROWS
                        act = _gelu_exact(pre[rl:rl + BF16_ROWS, lanes]).astype(BF16)
                        p_write[q * PEER_PIECE + rl:q * PEER_PIECE + rl + BF16_ROWS, lanes] = ws[al][g] * act

    @pl.when(n % 2 == 0)
    def _():
        step(p_even, p_odd)

    @pl.when(n % 2 == 1)
    def _():
        step(p_odd, p_even)

    @pl.when(jnp.logical_and(n >= 1, j_val == n_expert_tiles - 1))
    def _():
        gate2 = mod_ref[0, 5:6, :]
        out = x_ref[...] + gate2 * acc[...].T
        if final_norm:
            out = out * lax.rsqrt(jnp.mean(out * out, axis=-1, keepdims=True) + EPS) * fg_ref[...]
        o_ref[...] = out


def _peer_call(hT, e1, cnt, r2, e2, u_words, vT_words, x2d, mod_l, final_g, seq_len, final_norm):
    T, D = x2d.shape
    E = vT_words.shape[1]
    tc = PEER_COLS
    te = PEER_EXPERT_TILE
    steps_per_seq = seq_len // tc
    n_expert_tiles = E // te
    n_tiles = (T // tc) * n_expert_tiles
    tile = functools.partial(_peer_tile, n_tiles=n_tiles, n_expert_tiles=n_expert_tiles)
    packed_spec = pl.BlockSpec((PEER_HEADS, PEER_NKEYS // 2, tc), lambda n: (0, 0, tile(n, 0)[0]))
    key_spec = pl.BlockSpec((PEER_HEADS, te // PEER_NKEYS, tc), lambda n: (0, tile(n, 0)[1], tile(n, 0)[0]))
    return pl.pallas_call(
        functools.partial(_peer_kernel, final_norm=final_norm, n_tiles=n_tiles, n_expert_tiles=n_expert_tiles),
        grid=(n_tiles + 1,),
        in_specs=[
            pl.BlockSpec((D, tc), lambda n: (0, tile(n, 0)[0])),
            key_spec, key_spec, packed_spec, packed_spec,
            pl.BlockSpec((te // 2, D), lambda n: (tile(n, 0)[1], 0)),
            pl.BlockSpec((D // 2, te), lambda n: (0, tile(n, 1)[1])),
            pl.BlockSpec((tc, D), lambda n: (tile(n, 1)[0], 0)),
            pl.BlockSpec((1, N_MOD, D), lambda n: (tile(n, 1)[0] // steps_per_seq, 0, 0)),
            pl.BlockSpec((1, D), lambda n: (0, 0)),
        ],
        out_specs=pl.BlockSpec((tc, D), lambda n: (tile(n, 1)[0], 0)),
        out_shape=jax.ShapeDtypeStruct((T, D), F32),
        scratch_shapes=[
            pltpu.VMEM((D, tc), F32),
            pltpu.VMEM((te, tc), BF16),
            pltpu.VMEM((te, tc), BF16),
        ],
        compiler_params=pltpu.CompilerParams(
            dimension_semantics=("arbitrary",), vmem_limit_bytes=VMEM_LIMIT_BYTES),
        name="peer_experts",
    )(hT, e1, cnt, r2, e2, u_words, vT_words, x2d, mod_l, final_g.reshape(1, D))


def _pack_kernel(w_ref, o_ref, *, transpose):
    w = w_ref[...]
    if transpose:
        w = w.T
    o_ref[...] = pltpu.bitcast(w.astype(BF16), jnp.uint32)


def _pack_row_pairs(tables, layer, transpose):
    _, R, C = tables.shape
    blk = PACK_ROWS
    if transpose:
        out_shape, out_spec = (C // 2, R), pl.BlockSpec((C // 2, blk), lambda i: (0, i))
    else:
        out_shape, out_spec = (R // 2, C), pl.BlockSpec((blk // 2, C), lambda i: (i, 0))
    return pl.pallas_call(
        functools.partial(_pack_kernel, transpose=transpose),
        grid=(R // blk,),
        in_specs=[pl.BlockSpec((None, blk, C), lambda i: (layer, i, 0))],
        out_specs=out_spec,
        out_shape=jax.ShapeDtypeStruct(out_shape, jnp.uint32),
        compiler_params=pltpu.CompilerParams(
            dimension_semantics=("arbitrary",), vmem_limit_bytes=VMEM_LIMIT_BYTES),
        name="pack_table",
    )(tables)


@jax.jit
def _forward(x, c, ada_w, ada_b, norm1_g, w_in, b_in, conv_w, conv_b, conv_ln_g, conv_ln_b, w_conv_out,
             pool_w, pool_scale, w_out, b_out, norm2_g, peer_wq, peer_keys, peer_u, peer_v, final_g):
    B, S, D = x.shape
    L = ada_w.shape[0]
    c_pad = jnp.zeros((SUBLANES, D), F32).at[:B].set(c)
    mod = _ada_call(c_pad, ada_w, ada_b)[:, :B].reshape(L, B, N_MOD, D)
    for l in range(L):
        x = _mixer_call(x, mod[l], norm1_g[l], w_in[l], b_in[l], conv_w[l], conv_b[l], conv_ln_g[l],
                        conv_ln_b[l], w_conv_out[l], pool_w[l], pool_scale[l], w_out[l], b_out[l])
        x2d = x.reshape(B * S, D)
        hT, e1, cnt, r2, e2 = _route_call(x2d, mod[l], norm2_g[l], peer_wq[l].T.astype(BF16), peer_keys[l], S)
        x2d = _peer_call(hT, e1, cnt, r2, e2, _pack_row_pairs(peer_u, l, transpose=False),
                         _pack_row_pairs(peer_v, l, transpose=True), x2d, mod[l],
                         final_g, S, final_norm=(l == L - 1))
        x = x2d.reshape(B, S, D)
    return x


def kernel(x, c, ada_w, ada_b, norm1_g, w_in, b_in, conv_w, conv_b, conv_ln_g, conv_ln_b, w_conv_out, pool_w,
           pool_scale, w_out, b_out, norm2_g, peer_wq, peer_keys, peer_u, peer_v, final_g):
    return _forward(x, c, ada_w, ada_b, norm1_g, w_in, b_in, conv_w, conv_b, conv_ln_g, conv_ln_b, w_conv_out,
                    pool_w, pool_scale, w_out, b_out, norm2_g, peer_wq, peer_keys, peer_u, peer_v, final_g)
```

```python
import functools
import math

import jax
import jax.numpy as jnp
from jax import lax
from jax.experimental import pallas as pl
from jax.experimental.pallas import tpu as pltpu

F32 = jnp.float32
BF16 = jnp.bfloat16

EPS = 1e-6
CONV_KERNEL = 31
POOL_WINDOWS = (2, 4, 8, 16)
N_MOD = 6
PEER_HEADS = 8
PEER_NKEYS = 128
PEER_TOPK = 16

LANES = 128
SUBLANES = 8
BF16_ROWS = 16
VMEM_LIMIT_BYTES = 56 * 1024 * 1024

MIXER_ROWS = 512
CONV_HALO = 32
POOL_HALO = 16
CONV_ROWS = 64
CONV_LANES = 128
ROUTE_COLS = 512
ROUTE_CHUNKS_PER_ITER = 4
PEER_COLS = 512
PEER_EXPERT_TILE = 1024
PEER_PIECE = 256
PACK_ROWS = 1024

NEG_INF = float("-inf")


def _rmsnorm_mod(x, g, shift, scale):
    y = x * lax.rsqrt(jnp.mean(x * x, axis=-1, keepdims=True) + EPS) * g
    return y * (1.0 + scale) + shift


def _ada_kernel(c_ref, w_ref, b_ref, o_ref):
    c = c_ref[...]
    act = c * jax.nn.sigmoid(c)
    o_ref[0] = jnp.dot(act, w_ref[0], preferred_element_type=F32,
                       precision=lax.Precision.HIGHEST) + b_ref[0]


def _ada_call(c_pad, ada_w, ada_b):
    L, D, ND = ada_w.shape
    rows = c_pad.shape[0]
    return pl.pallas_call(
        _ada_kernel,
        grid=(L, ND // D),
        in_specs=[
            pl.BlockSpec((rows, D), lambda l, n: (0, 0)),
            pl.BlockSpec((1, D, D), lambda l, n: (l, 0, n)),
            pl.BlockSpec((1, 1, D), lambda l, n: (l, 0, n)),
        ],
        out_specs=pl.BlockSpec((1, rows, D), lambda l, n: (l, 0, n)),
        out_shape=jax.ShapeDtypeStruct((L, rows, ND), F32),
        compiler_params=pltpu.CompilerParams(
            dimension_semantics=("arbitrary", "arbitrary"), vmem_limit_bytes=VMEM_LIMIT_BYTES),
        name="ada_mod",
    )(c_pad, ada_w, ada_b.reshape(L, 1, ND))


def _mixer_kernel(x_ref, mod_ref, n1g_ref, win_ref, bin_ref, cw_ref, cb_ref, lng_ref, lnb_ref,
                  wco_ref, pw_ref, ps_ref, wout_ref, bout_ref, o_ref, ubuf, pbuf, cbuf):
    ts, D = cbuf.shape
    s = pl.program_id(1)

    @pl.when(s == 0)
    def _():
        ubuf[0:CONV_HALO, :] = jnp.zeros((CONV_HALO, D), F32)
        pbuf[0:POOL_HALO, :] = jnp.zeros((POOL_HALO, D), F32)

    @pl.when(s > 0)
    def _():
        ubuf[0:CONV_HALO, :] = ubuf[ts:ts + CONV_HALO, :]
        pbuf[0:POOL_HALO, :] = pbuf[ts:ts + POOL_HALO, :]

    x = x_ref[0]
    shift1 = mod_ref[0, 0:1, :]
    scale1 = mod_ref[0, 1:2, :]
    gate1 = mod_ref[0, 2:3, :]
    h = _rmsnorm_mod(x, n1g_ref[...], shift1, scale1).astype(BF16)

    def proj(j):
        cols = slice(j * D, (j + 1) * D)
        return jnp.dot(h, win_ref[:, cols], preferred_element_type=F32) + bin_ref[:, cols]

    za = proj(0)
    zb = proj(1)
    ubuf[CONV_HALO:CONV_HALO + ts, :] = za * jax.nn.sigmoid(zb)
    zp = proj(2)
    pbuf[POOL_HALO:POOL_HALO + ts, :] = zp

    def conv_block(i, carry):
        r0 = pl.multiple_of(i * CONV_ROWS, CONV_ROWS)
        for lb in range(D // CONV_LANES):
            lanes = slice(lb * CONV_LANES, (lb + 1) * CONV_LANES)
            win = ubuf[pl.ds(r0, CONV_ROWS + CONV_HALO), lanes]
            acc = jnp.broadcast_to(cb_ref[:, lanes], (CONV_ROWS, CONV_LANES))
            for r in range(SUBLANES):
                shifted = win if r == 0 else pltpu.roll(win, r, axis=0)
                for q in range(CONV_HALO // SUBLANES):
                    lag = SUBLANES * q + r
                    if lag >= CONV_KERNEL:
                        continue
                    k = CONV_KERNEL - 1 - lag
                    x0 = CONV_HALO - SUBLANES * q
                    acc = acc + cw_ref[k:k + 1, lanes] * shifted[x0:x0 + CONV_ROWS, :]
            cbuf[pl.ds(r0, CONV_ROWS), lanes] = acc
        return carry

    lax.fori_loop(0, ts // CONV_ROWS, conv_block, 0)

    u = cbuf[...]
    mu = jnp.mean(u, axis=-1, keepdims=True)
    uc = u - mu
    var = jnp.mean(uc * uc, axis=-1, keepdims=True)
    un = uc * lax.rsqrt(var + EPS) * lng_ref[...] + lnb_ref[...]
    act = (un * jax.nn.sigmoid(un)).astype(BF16)
    conv_out = jnp.dot(act, wco_ref[...], preferred_element_type=F32)

    n_groups = len(POOL_WINDOWS)
    pg = D // n_groups
    frame = lax.broadcasted_iota(jnp.int32, (ts, pg), 0) + (s * ts + 1)
    pool_parts = []
    for g, w in enumerate(POOL_WINDOWS):
        cols = slice(g * pg, (g + 1) * pg)
        tot = pbuf[POOL_HALO:POOL_HALO + ts, cols]
        for d in range(1, w):
            tot = tot + pbuf[POOL_HALO - d:POOL_HALO - d + ts, cols]
        cnt = jnp.minimum(frame, w).astype(F32)
        pooled = tot / cnt - zp[:, cols]
        pool_parts.append(jnp.dot(pooled.astype(BF16), pw_ref[g], preferred_element_type=F32))
    pool_out = jnp.concatenate(pool_parts, axis=-1) * ps_ref[...]

    ga = jax.nn.sigmoid(proj(3))
    gb = jax.nn.sigmoid(proj(4))
    y = (ga * conv_out + gb * pool_out).astype(BF16)
    o_ref[0] = x + gate1 * (jnp.dot(y, wout_ref[...], preferred_element_type=F32) + bout_ref[...])


def _mixer_call(x, mod_l, n1g, w_in, b_in, conv_w, conv_b, ln_g, ln_b, w_co, pool_w, pool_scale, w_out, b_out):
    B, S, D = x.shape
    ts = MIXER_ROWS
    d_in = w_in.shape[1]
    n_groups, pg, _ = pool_w.shape
    const2 = lambda b, s: (0, 0)
    const3 = lambda b, s: (0, 0, 0)
    row = lambda v: v.reshape(1, -1)
    return pl.pallas_call(
        _mixer_kernel,
        grid=(B, S // ts),
        in_specs=[
            pl.BlockSpec((1, ts, D), lambda b, s: (b, s, 0)),
            pl.BlockSpec((1, N_MOD, D), lambda b, s: (b, 0, 0)),
            pl.BlockSpec((1, D), const2),
            pl.BlockSpec((D, d_in), const2),
            pl.BlockSpec((1, d_in), const2),
            pl.BlockSpec((CONV_KERNEL, D), const2),
            pl.BlockSpec((1, D), const2),
            pl.BlockSpec((1, D), const2),
            pl.BlockSpec((1, D), const2),
            pl.BlockSpec((D, D), const2),
            pl.BlockSpec((n_groups, pg, pg), const3),
            pl.BlockSpec((1, D), const2),
            pl.BlockSpec((D, D), const2),
            pl.BlockSpec((1, D), const2),
        ],
        out_specs=pl.BlockSpec((1, ts, D), lambda b, s: (b, s, 0)),
        out_shape=jax.ShapeDtypeStruct((B, S, D), F32),
        scratch_shapes=[
            pltpu.VMEM((CONV_HALO + ts, D), F32),
            pltpu.VMEM((POOL_HALO + ts, D), F32),
            pltpu.VMEM((ts, D), F32),
        ],
        compiler_params=pltpu.CompilerParams(
            dimension_semantics=("arbitrary", "arbitrary"), vmem_limit_bytes=VMEM_LIMIT_BYTES),
        name="mixer",
    )(x, mod_l, row(n1g), w_in.astype(BF16), row(b_in), conv_w, row(conv_b), row(ln_g), row(ln_b),
      w_co.astype(BF16), pool_w.astype(BF16), row(pool_scale), w_out.astype(BF16), row(b_out))


def _bf16_pair_bits(x):
    hi = pltpu.bitcast(x.astype(BF16).astype(F32), jnp.uint32)
    return hi | (hi >> 16)


def _row_as_packed_bf16(ref, hd, row, lanes):
    words = jnp.broadcast_to(ref[hd, row:row + 1, lanes], (SUBLANES, LANES))
    return pltpu.bitcast(words, BF16)


def _top_ranks(s):
    rank = jnp.full(s.shape, float(PEER_TOPK), F32)
    cur = s
    tops = []
    for r in range(PEER_TOPK):
        m = jnp.max(cur, axis=0, keepdims=True)
        hit = cur == m
        rank = jnp.where(hit, float(r), rank)
        cur = jnp.where(hit, NEG_INF, cur)
        tops.append(m)
    return rank, tops


def _route_kernel(x_ref, mod_ref, n2g_ref, wqT_ref, keys_ref, hT_ref, e1_ref, cnt_ref, r2_ref, e2_ref,
                  qT_scr, s_scr):
    tc = x_ref.shape[0]
    shift2 = mod_ref[0, 3:4, :]
    scale2 = mod_ref[0, 4:5, :]
    h = _rmsnorm_mod(x_ref[...], n2g_ref[...], shift2, scale2)
    hT = h.T.astype(BF16)
    hT_ref[...] = hT
    qT_scr[...] = jnp.dot(wqT_ref[...], hT, preferred_element_type=F32)

    row16 = lax.broadcasted_iota(jnp.int32, (PEER_TOPK, LANES), 0)
    row8 = lax.broadcasted_iota(jnp.int32, (SUBLANES, LANES), 0)

    def head_body(hd, carry):
        for p in range(2):
            k0 = pl.multiple_of(hd * (2 * PEER_NKEYS) + p * PEER_NKEYS, PEER_NKEYS)
            q_hp = qT_scr[pl.ds(k0, PEER_NKEYS), :]
            s_scr[p] = jnp.dot(keys_ref[hd, p], q_hp, preferred_element_type=F32,
                               precision=lax.Precision.HIGHEST)

        def chunk_pair_body(c2, carry2):
            for sub in range(ROUTE_CHUNKS_PER_ITER):
                route_chunk(c2 * ROUTE_CHUNKS_PER_ITER + sub)
            return carry2

        def route_chunk(c):
            l0 = pl.multiple_of(c * LANES, LANES)
            s1 = s_scr[0, :, pl.ds(l0, LANES)]
            s2 = s_scr[1, :, pl.ds(l0, LANES)]
            rank1, top1 = _top_ranks(s1)
            rank2, top2 = _top_ranks(s2)
            top1a = jnp.zeros((PEER_TOPK, LANES), F32)
            top2a = jnp.zeros((PEER_TOPK, LANES), F32)
            for r in range(PEER_TOPK):
                top1a = jnp.where(row16 == r, top1[r], top1a)
                top2a = jnp.where(row16 == r, top2[r], top2a)

            lo2, hi2 = top2a[0:SUBLANES], top2a[SUBLANES:]
            lo1, hi1 = top1a[0:SUBLANES], top1a[SUBLANES:]
            by_j = [top1[0] + lo2, top1[0] + hi2, top1[1] + lo2]
            by_j += [jnp.where(row8 < PEER_TOPK // (i + 1), top1[i] + lo2, NEG_INF) for i in (2, 3, 4)]
            by_i = [jnp.where(row8 >= 5, lo1 + top2[j], NEG_INF) for j in (0, 1)]
            by_i8 = hi1 + top2[0]
            slabs = by_j + by_i + [by_i8]
            cur = list(slabs)
            thr = None
            for r in range(PEER_TOPK):
                m8 = cur[0]
                for sl in cur[1:]:
                    m8 = jnp.maximum(m8, sl)
                thr = jnp.max(m8, axis=0, keepdims=True)
                if r + 1 < PEER_TOPK:
                    cur = [jnp.where(sl == thr, NEG_INF, sl) for sl in cur]

            cmax = top1[0] + top2[0]
            ones = [jnp.where(slab >= thr, 1.0, 0.0) for slab in slabs]
            zsum = jnp.zeros((1, LANES), F32)
            for slab, one in zip(slabs, ones):
                zsum = zsum + jnp.sum(one * jnp.exp(slab - cmax), axis=0, keepdims=True)
            col_sum = lambda v: jnp.sum(v, axis=0, keepdims=True)
            per_i = [col_sum(ones[0]) + col_sum(ones[1])] + [col_sum(ones[k]) for k in (2, 3, 4, 5)]
            tail_lo = ones[6] + ones[7]
            per_i += [tail_lo[i:i + 1] for i in (5, 6, 7)]
            per_i += [ones[8][i:i + 1] for i in range(SUBLANES)]
            cnt = jnp.zeros(s1.shape, F32)
            for i, n_i in enumerate(per_i):
                cnt = jnp.where(rank1 == float(i), n_i, cnt)

            e1_ref[hd, :, pl.ds(l0, LANES)] = _bf16_pair_bits(jnp.exp(s1 - top1[0]))
            cnt_ref[hd, :, pl.ds(l0, LANES)] = _bf16_pair_bits(cnt)
            r2_ref[hd, :, pl.ds(l0, LANES)] = pltpu.bitcast(rank2.astype(BF16), jnp.uint32)
            e2_ref[hd, :, pl.ds(l0, LANES)] = pltpu.bitcast((jnp.exp(s2 - top2[0]) / zsum).astype(BF16), jnp.uint32)

        lax.fori_loop(0, tc // (LANES * ROUTE_CHUNKS_PER_ITER), chunk_pair_body, 0)
        return carry

    lax.fori_loop(0, PEER_HEADS, head_body, 0)


def _route_call(x2d, mod_l, n2g, wqT, keys, seq_len):
    T, D = x2d.shape
    tc = ROUTE_COLS
    HK = wqT.shape[0]
    steps_per_seq = seq_len // tc
    route_shape = jax.ShapeDtypeStruct((PEER_HEADS, PEER_NKEYS, T), jnp.uint32)
    route_bf = jax.ShapeDtypeStruct((PEER_HEADS, PEER_NKEYS // 2, T), jnp.uint32)
    route_spec = pl.BlockSpec((PEER_HEADS, PEER_NKEYS, tc), lambda i: (0, 0, i))
    pair_spec = pl.BlockSpec((PEER_HEADS, PEER_NKEYS // 2, tc), lambda i: (0, 0, i))
    return pl.pallas_call(
        _route_kernel,
        grid=(T // tc,),
        in_specs=[
            pl.BlockSpec((tc, D), lambda i: (i, 0)),
            pl.BlockSpec((1, N_MOD, D), lambda i: (i // steps_per_seq, 0, 0)),
            pl.BlockSpec((1, D), lambda i: (0, 0)),
            pl.BlockSpec((HK, D), lambda i: (0, 0)),
            pl.BlockSpec(keys.shape, lambda i: (0, 0, 0, 0)),
        ],
        out_specs=[pl.BlockSpec((D, tc), lambda i: (0, i)), route_spec, route_spec, pair_spec, pair_spec],
        out_shape=[jax.ShapeDtypeStruct((D, T), BF16), route_shape, route_shape, route_bf, route_bf],
        scratch_shapes=[
            pltpu.VMEM((HK, tc), F32),
            pltpu.VMEM((2, PEER_NKEYS, tc), F32),
        ],
        compiler_params=pltpu.CompilerParams(
            dimension_semantics=("arbitrary",), vmem_limit_bytes=VMEM_LIMIT_BYTES),
        name="peer_route",
    )(x2d, mod_l, n2g.reshape(1, D), wqT, keys)


def _gelu_exact(x):
    return 0.5 * x * (1.0 + lax.erf(x * (1.0 / math.sqrt(2.0))))


def _peer_tile(n, lag, n_tiles, n_expert_tiles):
    t = jnp.clip(n - lag, 0, n_tiles - 1)
    return t // n_expert_tiles, t % n_expert_tiles


def _peer_kernel(hT_ref, e1_ref, cnt_ref, r2_ref, e2_ref, u_ref, vT_ref, x_ref, mod_ref, fg_ref, o_ref,
                 acc, p_even, p_odd, *, final_norm, n_tiles, n_expert_tiles):
    te = vT_ref.shape[1]
    tc = hT_ref.shape[1]
    n = pl.program_id(0)
    _, j_val = _peer_tile(n, 1, n_tiles, n_expert_tiles)

    @pl.when(n == 0)
    def _():
        for ref in (p_even, p_odd):
            ref[...] = jnp.zeros(ref.shape, ref.dtype)

    @pl.when(j_val == 0)
    def _():
        acc[...] = jnp.zeros(acc.shape, F32)

    a_per_piece = PEER_PIECE // PEER_NKEYS
    n_pieces = te // PEER_PIECE
    n_chunks = tc // LANES
    groups = PEER_NKEYS // BF16_ROWS
    zero_bf = jnp.zeros((BF16_ROWS, LANES), BF16)

    def step(p_write, p_read):
        def scores(q):
            words_q = slice(q * PEER_PIECE // 2, (q + 1) * PEER_PIECE // 2)
            return jnp.dot(pltpu.bitcast(u_ref[words_q, :], BF16), hT_ref[...],
                           preferred_element_type=F32)

        acc[...] += jnp.dot(pltpu.bitcast(vT_ref[...], BF16), p_read[...], preferred_element_type=F32)
        for q in range(n_pieces):
            pre = scores(q)
            for c in range(n_chunks):
                lanes = slice(c * LANES, (c + 1) * LANES)
                ws = [[None] * groups for _ in range(a_per_piece)]
                for hd in range(PEER_HEADS):
                    for al in range(a_per_piece):
                        cnt_row = _row_as_packed_bf16(cnt_ref, hd, q * a_per_piece + al, lanes)
                        e1_row = _row_as_packed_bf16(e1_ref, hd, q * a_per_piece + al, lanes)
                        for g in range(groups):
                            words = slice(g * SUBLANES, (g + 1) * SUBLANES)
                            r2 = pltpu.bitcast(r2_ref[hd, words, lanes], BF16)
                            e2 = pltpu.bitcast(e2_ref[hd, words, lanes], BF16)
                            term = jnp.where(r2 < cnt_row, e2, zero_bf) * e1_row
                            ws[al][g] = term if hd == 0 else ws[al][g] + term
                for al in range(a_per_piece):
                    for g in range(groups):
                        rl = al * PEER_NKEYS + g * BF16_ROWS
                        act = _gelu_exact(pre[rl:rl + BF16_ROWS, lanes]).astype(BF16)
                        p_write[q * PEER_PIECE + rl:q * PEER_PIECE + rl + BF16_ROWS, lanes] = ws[al][g] * act

    @pl.when(n % 2 == 0)
    def _():
        step(p_even, p_odd)

    @pl.when(n % 2 == 1)
    def _():
        step(p_odd, p_even)

    @pl.when(jnp.logical_and(n >= 1, j_val == n_expert_tiles - 1))
    def _():
        gate2 = mod_ref[0, 5:6, :]
        out = x_ref[...] + gate2 * acc[...].T
        if final_norm:
            out = out * lax.rsqrt(jnp.mean(out * out, axis=-1, keepdims=True) + EPS) * fg_ref[...]
        o_ref[...] = out


def _peer_call(hT, e1, cnt, r2, e2, u_words, vT_words, x2d, mod_l, final_g, seq_len, final_norm):
    T, D = x2d.shape
    E = vT_words.shape[1]
    tc = PEER_COLS
    te = PEER_EXPERT_TILE
    steps_per_seq = seq_len // tc
    n_expert_tiles = E // te
    n_tiles = (T // tc) * n_expert_tiles
    tile = functools.partial(_peer_tile, n_tiles=n_tiles, n_expert_tiles=n_expert_tiles)
    packed_spec = pl.BlockSpec((PEER_HEADS, PEER_NKEYS // 2, tc), lambda n: (0, 0, tile(n, 0)[0]))
    key_spec = pl.BlockSpec((PEER_HEADS, te // PEER_NKEYS, tc), lambda n: (0, tile(n, 0)[1], tile(n, 0)[0]))
    return pl.pallas_call(
        functools.partial(_peer_kernel, final_norm=final_norm, n_tiles=n_tiles, n_expert_tiles=n_expert_tiles),
        grid=(n_tiles + 1,),
        in_specs=[
            pl.BlockSpec((D, tc), lambda n: (0, tile(n, 0)[0])),
            key_spec, key_spec, packed_spec, packed_spec,
            pl.BlockSpec((te // 2, D), lambda n: (tile(n, 0)[1], 0)),
            pl.BlockSpec((D // 2, te), lambda n: (0, tile(n, 1)[1])),
            pl.BlockSpec((tc, D), lambda n: (tile(n, 1)[0], 0)),
            pl.BlockSpec((1, N_MOD, D), lambda n: (tile(n, 1)[0] // steps_per_seq, 0, 0)),
            pl.BlockSpec((1, D), lambda n: (0, 0)),
        ],
        out_specs=pl.BlockSpec((tc, D), lambda n: (tile(n, 1)[0], 0)),
        out_shape=jax.ShapeDtypeStruct((T, D), F32),
        scratch_shapes=[
            pltpu.VMEM((D, tc), F32),
            pltpu.VMEM((te, tc), BF16),
            pltpu.VMEM((te, tc), BF16),
        ],
        compiler_params=pltpu.CompilerParams(
            dimension_semantics=("arbitrary",), vmem_limit_bytes=VMEM_LIMIT_BYTES),
        name="peer_experts",
    )(hT, e1, cnt, r2, e2, u_words, vT_words, x2d, mod_l, final_g.reshape(1, D))


def _pack_kernel(w_ref, o_ref, *, transpose):
    w = w_ref[...]
    if transpose:
        w = w.T
    o_ref[...] = pltpu.bitcast(w.astype(BF16), jnp.uint32)


def _pack_row_pairs(tables, layer, transpose):
    _, R, C = tables.shape
    blk = PACK_ROWS
    if transpose:
        out_shape, out_spec = (C // 2, R), pl.BlockSpec((C // 2, blk), lambda i: (0, i))
    else:
        out_shape, out_spec = (R // 2, C), pl.BlockSpec((blk // 2, C), lambda i: (i, 0))
    return pl.pallas_call(
        functools.partial(_pack_kernel, transpose=transpose),
        grid=(R // blk,),
        in_specs=[pl.BlockSpec((None, blk, C), lambda i: (layer, i, 0))],
        out_specs=out_spec,
        out_shape=jax.ShapeDtypeStruct(out_shape, jnp.uint32),
        compiler_params=pltpu.CompilerParams(
            dimension_semantics=("arbitrary",), vmem_limit_bytes=VMEM_LIMIT_BYTES),
        name="pack_table",
    )(tables)


@jax.jit
def _forward(x, c, ada_w, ada_b, norm1_g, w_in, b_in, conv_w, conv_b, conv_ln_g, conv_ln_b, w_conv_out,
             pool_w, pool_scale, w_out, b_out, norm2_g, peer_wq, peer_keys, peer_u, peer_v, final_g):
    B, S, D = x.shape
    L = ada_w.shape[0]
    c_pad = jnp.zeros((SUBLANES, D), F32).at[:B].set(c)
    mod = _ada_call(c_pad, ada_w, ada_b)[:, :B].reshape(L, B, N_MOD, D)
    for l in range(L):
        x = _mixer_call(x, mod[l], norm1_g[l], w_in[l], b_in[l], conv_w[l], conv_b[l], conv_ln_g[l],
                        conv_ln_b[l], w_conv_out[l], pool_w[l], pool_scale[l], w_out[l], b_out[l])
        x2d = x.reshape(B * S, D)
        hT, e1, cnt, r2, e2 = _route_call(x2d, mod[l], norm2_g[l], peer_wq[l].T.astype(BF16), peer_keys[l], S)
        x2d = _peer_call(hT, e1, cnt, r2, e2, _pack_row_pairs(peer_u, l, transpose=False),
                         _pack_row_pairs(peer_v, l, transpose=True), x2d, mod[l],
                         final_g, S, final_norm=(l == L - 1))
        x = x2d.reshape(B, S, D)
    return x


def kernel(x, c, ada_w, ada_b, norm1_g, w_in, b_in, conv_w, conv_b, conv_ln_g, conv_ln_b, w_conv_out, pool_w,
           pool_scale, w_out, b_out, norm2_g, peer_wq, peer_keys, peer_u, peer_v, final_g):
    return _forward(x, c, ada_w, ada_b, norm1_g, w_in, b_in, conv_w, conv_b, conv_ln_g, conv_ln_b, w_conv_out,
                    pool_w, pool_scale, w_out, b_out, norm2_g, peer_wq, peer_keys, peer_u, peer_v, final_g)
```

```python
import functools
import math

import jax
import jax.numpy as jnp
from jax import lax
from jax.experimental import pallas as pl
from jax.experimental.pallas import tpu as pltpu

F32 = jnp.float32
BF16 = jnp.bfloat16

EPS = 1e-6
CONV_KERNEL = 31
POOL_WINDOWS = (2, 4, 8, 16)
N_MOD = 6
PEER_HEADS = 8
PEER_NKEYS = 128
PEER_TOPK = 16

LANES = 128
SUBLANES = 8
BF16_ROWS = 16
VMEM_LIMIT_BYTES = 56 * 1024 * 1024

MIXER_ROWS = 512
CONV_HALO = 32
POOL_HALO = 16
CONV_ROWS = 64
CONV_LANES = 128
ROUTE_COLS = 512
ROUTE_CHUNKS_PER_ITER = 4
PEER_COLS = 512
PEER_EXPERT_TILE = 1024
PEER_PIECE = 256
PACK_ROWS = 1024

NEG_INF = float("-inf")


def _rmsnorm_mod(x, g, shift, scale):
    y = x * lax.rsqrt(jnp.mean(x * x, axis=-1, keepdims=True) + EPS) * g
    return y * (1.0 + scale) + shift


def _ada_kernel(c_ref, w_ref, b_ref, o_ref):
    c = c_ref[...]
    act = c * jax.nn.sigmoid(c)
    o_ref[0] = jnp.dot(act, w_ref[0], preferred_element_type=F32,
                       precision=lax.Precision.HIGHEST) + b_ref[0]


def _ada_call(c_pad, ada_w, ada_b):
    L, D, ND = ada_w.shape
    rows = c_pad.shape[0]
    return pl.pallas_call(
        _ada_kernel,
        grid=(L, ND // D),
        in_specs=[
            pl.BlockSpec((rows, D), lambda l, n: (0, 0)),
            pl.BlockSpec((1, D, D), lambda l, n: (l, 0, n)),
            pl.BlockSpec((1, 1, D), lambda l, n: (l, 0, n)),
        ],
        out_specs=pl.BlockSpec((1, rows, D), lambda l, n: (l, 0, n)),
        out_shape=jax.ShapeDtypeStruct((L, rows, ND), F32),
        compiler_params=pltpu.CompilerParams(
            dimension_semantics=("arbitrary", "arbitrary"), vmem_limit_bytes=VMEM_LIMIT_BYTES),
        name="ada_mod",
    )(c_pad, ada_w, ada_b.reshape(L, 1, ND))


def _mixer_kernel(x_ref, mod_ref, n1g_ref, win_ref, bin_ref, cw_ref, cb_ref, lng_ref, lnb_ref,
                  wco_ref, pw_ref, ps_ref, wout_ref, bout_ref, o_ref, ubuf, pbuf, cbuf):
    ts, D = cbuf.shape
    s = pl.program_id(1)

    @pl.when(s == 0)
    def _():
        ubuf[0:CONV_HALO, :] = jnp.zeros((CONV_HALO, D), F32)
        pbuf[0:POOL_HALO, :] = jnp.zeros((POOL_HALO, D), F32)

    @pl.when(s > 0)
    def _():
        ubuf[0:CONV_HALO, :] = ubuf[ts:ts + CONV_HALO, :]
        pbuf[0:POOL_HALO, :] = pbuf[ts:ts + POOL_HALO, :]

    x = x_ref[0]
    shift1 = mod_ref[0, 0:1, :]
    scale1 = mod_ref[0, 1:2, :]
    gate1 = mod_ref[0, 2:3, :]
    h = _rmsnorm_mod(x, n1g_ref[...], shift1, scale1).astype(BF16)

    def proj(j):
        cols = slice(j * D, (j + 1) * D)
        return jnp.dot(h, win_ref[:, cols], preferred_element_type=F32) + bin_ref[:, cols]

    za = proj(0)
    zb = proj(1)
    ubuf[CONV_HALO:CONV_HALO + ts, :] = za * jax.nn.sigmoid(zb)
    zp = proj(2)
    pbuf[POOL_HALO:POOL_HALO + ts, :] = zp

    def conv_block(i, carry):
        r0 = pl.multiple_of(i * CONV_ROWS, CONV_ROWS)
        for lb in range(D // CONV_LANES):
            lanes = slice(lb * CONV_LANES, (lb + 1) * CONV_LANES)
            win = ubuf[pl.ds(r0, CONV_ROWS + CONV_HALO), lanes]
            acc = jnp.broadcast_to(cb_ref[:, lanes], (CONV_ROWS, CONV_LANES))
            for r in range(SUBLANES):
                shifted = win if r == 0 else pltpu.roll(win, r, axis=0)
                for q in range(CONV_HALO // SUBLANES):
                    lag = SUBLANES * q + r
                    if lag >= CONV_KERNEL:
                        continue
                    k = CONV_KERNEL - 1 - lag
                    x0 = CONV_HALO - SUBLANES * q
                    acc = acc + cw_ref[k:k + 1, lanes] * shifted[x0:x0 + CONV_ROWS, :]
            cbuf[pl.ds(r0, CONV_ROWS), lanes] = acc
        return carry

    lax.fori_loop(0, ts // CONV_ROWS, conv_block, 0)

    u = cbuf[...]
    mu = jnp.mean(u, axis=-1, keepdims=True)
    uc = u - mu
    var = jnp.mean(uc * uc, axis=-1, keepdims=True)
    un = uc * lax.rsqrt(var + EPS) * lng_ref[...] + lnb_ref[...]
    act = (un * jax.nn.sigmoid(un)).astype(BF16)
    conv_out = jnp.dot(act, wco_ref[...], preferred_element_type=F32)

    n_groups = len(POOL_WINDOWS)
    pg = D // n_groups
    frame = lax.broadcasted_iota(jnp.int32, (ts, pg), 0) + (s * ts + 1)
    pool_parts = []
    for g, w in enumerate(POOL_WINDOWS):
        cols = slice(g * pg, (g + 1) * pg)
        tot = pbuf[POOL_HALO:POOL_HALO + ts, cols]
        for d in range(1, w):
            tot = tot + pbuf[POOL_HALO - d:POOL_HALO - d + ts, cols]
        cnt = jnp.minimum(frame, w).astype(F32)
        pooled = tot / cnt - zp[:, cols]
        pool_parts.append(jnp.dot(pooled.astype(BF16), pw_ref[g], preferred_element_type=F32))
    pool_out = jnp.concatenate(pool_parts, axis=-1) * ps_ref[...]

    ga = jax.nn.sigmoid(proj(3))
    gb = jax.nn.sigmoid(proj(4))
    y = (ga * conv_out + gb * pool_out).astype(BF16)
    o_ref[0] = x + gate1 * (jnp.dot(y, wout_ref[...], preferred_element_type=F32) + bout_ref[...])


def _mixer_call(x, mod_l, n1g, w_in, b_in, conv_w, conv_b, ln_g, ln_b, w_co, pool_w, pool_scale, w_out, b_out):
    B, S, D = x.shape
    ts = MIXER_ROWS
    d_in = w_in.shape[1]
    n_groups, pg, _ = pool_w.shape
    const2 = lambda b, s: (0, 0)
    const3 = lambda b, s: (0, 0, 0)
    row = lambda v: v.reshape(1, -1)
    return pl.pallas_call(
        _mixer_kernel,
        grid=(B, S // ts),
        in_specs=[
            pl.BlockSpec((1, ts, D), lambda b, s: (b, s, 0)),
            pl.BlockSpec((1, N_MOD, D), lambda b, s: (b, 0, 0)),
            pl.BlockSpec((1, D), const2),
            pl.BlockSpec((D, d_in), const2),
            pl.BlockSpec((1, d_in), const2),
            pl.BlockSpec((CONV_KERNEL, D), const2),
            pl.BlockSpec((1, D), const2),
            pl.BlockSpec((1, D), const2),
            pl.BlockSpec((1, D), const2),
            pl.BlockSpec((D, D), const2),
            pl.BlockSpec((n_groups, pg, pg), const3),
            pl.BlockSpec((1, D), const2),
            pl.BlockSpec((D, D), const2),
            pl.BlockSpec((1, D), const2),
        ],
        out_specs=pl.BlockSpec((1, ts, D), lambda b, s: (b, s, 0)),
        out_shape=jax.ShapeDtypeStruct((B, S, D), F32),
        scratch_shapes=[
            pltpu.VMEM((CONV_HALO + ts, D), F32),
            pltpu.VMEM((POOL_HALO + ts, D), F32),
            pltpu.VMEM((ts, D), F32),
        ],
        compiler_params=pltpu.CompilerParams(
            dimension_semantics=("arbitrary", "arbitrary"), vmem_limit_bytes=VMEM_LIMIT_BYTES),
        name="mixer",
    )(x, mod_l, row(n1g), w_in.astype(BF16), row(b_in), conv_w, row(conv_b), row(ln_g), row(ln_b),
      w_co.astype(BF16), pool_w.astype(BF16), row(pool_scale), w_out.astype(BF16), row(b_out))


def _bf16_pair_bits(x):
    hi = pltpu.bitcast(x.astype(BF16).astype(F32), jnp.uint32)
    return hi | (hi >> 16)


def _row_as_packed_bf16(ref, hd, row, lanes):
    words = jnp.broadcast_to(ref[hd, row:row + 1, lanes], (SUBLANES, LANES))
    return pltpu.bitcast(words, BF16)


def _top_ranks(s):
    rank = jnp.full(s.shape, float(PEER_TOPK), F32)
    cur = s
    tops = []
    for r in range(PEER_TOPK):
        m = jnp.max(cur, axis=0, keepdims=True)
        hit = cur == m
        rank = jnp.where(hit, float(r), rank)
        cur = jnp.where(hit, NEG_INF, cur)
        tops.append(m)
    return rank, tops


def _route_kernel(x_ref, mod_ref, n2g_ref, wqT_ref, keys_ref, hT_ref, e1_ref, cnt_ref, r2_ref, e2_ref,
                  qT_scr, s_scr):
    tc = x_ref.shape[0]
    shift2 = mod_ref[0, 3:4, :]
    scale2 = mod_ref[0, 4:5, :]
    h = _rmsnorm_mod(x_ref[...], n2g_ref[...], shift2, scale2)
    hT = h.T.astype(BF16)
    hT_ref[...] = hT
    qT_scr[...] = jnp.dot(wqT_ref[...], hT, preferred_element_type=F32)

    row16 = lax.broadcasted_iota(jnp.int32, (PEER_TOPK, LANES), 0)
    row8 = lax.broadcasted_iota(jnp.int32, (SUBLANES, LANES), 0)

    def head_body(hd, carry):
        for p in range(2):
            k0 = pl.multiple_of(hd * (2 * PEER_NKEYS) + p * PEER_NKEYS, PEER_NKEYS)
            q_hp = qT_scr[pl.ds(k0, PEER_NKEYS), :]
            s_scr[p] = jnp.dot(keys_ref[hd, p].astype(BF16), q_hp.astype(BF16),
                               preferred_element_type=F32)

        def chunk_pair_body(c2, carry2):
            for sub in range(ROUTE_CHUNKS_PER_ITER):
                route_chunk(c2 * ROUTE_CHUNKS_PER_ITER + sub)
            return carry2

        def route_chunk(c):
            l0 = pl.multiple_of(c * LANES, LANES)
            s1 = s_scr[0, :, pl.ds(l0, LANES)]
            s2 = s_scr[1, :, pl.ds(l0, LANES)]
            rank1, top1 = _top_ranks(s1)
            rank2, top2 = _top_ranks(s2)
            top1a = jnp.zeros((PEER_TOPK, LANES), F32)
            top2a = jnp.zeros((PEER_TOPK, LANES), F32)
            for r in range(PEER_TOPK):
                top1a = jnp.where(row16 == r, top1[r], top1a)
                top2a = jnp.where(row16 == r, top2[r], top2a)

            lo2, hi2 = top2a[0:SUBLANES], top2a[SUBLANES:]
            lo1, hi1 = top1a[0:SUBLANES], top1a[SUBLANES:]
            by_j = [top1[0] + lo2, top1[0] + hi2, top1[1] + lo2]
            by_j += [jnp.where(row8 < PEER_TOPK // (i + 1), top1[i] + lo2, NEG_INF) for i in (2, 3, 4)]
            by_i = [jnp.where(row8 >= 5, lo1 + top2[j], NEG_INF) for j in (0, 1)]
            by_i8 = hi1 + top2[0]
            slabs = by_j + by_i + [by_i8]
            cur = list(slabs)
            thr = None
            for r in range(PEER_TOPK):
                m8 = cur[0]
                for sl in cur[1:]:
                    m8 = jnp.maximum(m8, sl)
                thr = jnp.max(m8, axis=0, keepdims=True)
                if r + 1 < PEER_TOPK:
                    cur = [jnp.where(sl == thr, NEG_INF, sl) for sl in cur]

            cmax = top1[0] + top2[0]
            ones = [jnp.where(slab >= thr, 1.0, 0.0) for slab in slabs]
            zsum = jnp.zeros((1, LANES), F32)
            for slab, one in zip(slabs, ones):
                zsum = zsum + jnp.sum(one * jnp.exp(slab - cmax), axis=0, keepdims=True)
            col_sum = lambda v: jnp.sum(v, axis=0, keepdims=True)
            per_i = [col_sum(ones[0]) + col_sum(ones[1])] + [col_sum(ones[k]) for k in (2, 3, 4, 5)]
            tail_lo = ones[6] + ones[7]
            per_i += [tail_lo[i:i + 1] for i in (5, 6, 7)]
            per_i += [ones[8][i:i + 1] for i in range(SUBLANES)]
            cnt = jnp.zeros(s1.shape, F32)
            for i, n_i in enumerate(per_i):
                cnt = jnp.where(rank1 == float(i), n_i, cnt)

            e1_ref[hd, :, pl.ds(l0, LANES)] = _bf16_pair_bits(jnp.exp(s1 - top1[0]))
            cnt_ref[hd, :, pl.ds(l0, LANES)] = _bf16_pair_bits(cnt)
            r2_ref[hd, :, pl.ds(l0, LANES)] = pltpu.bitcast(rank2.astype(BF16), jnp.uint32)
            e2_ref[hd, :, pl.ds(l0, LANES)] = pltpu.bitcast((jnp.exp(s2 - top2[0]) / zsum).astype(BF16), jnp.uint32)

        lax.fori_loop(0, tc // (LANES * ROUTE_CHUNKS_PER_ITER), chunk_pair_body, 0)
        return carry

    lax.fori_loop(0, PEER_HEADS, head_body, 0)


def _route_call(x2d, mod_l, n2g, wqT, keys, seq_len):
    T, D = x2d.shape
    tc = ROUTE_COLS
    HK = wqT.shape[0]
    steps_per_seq = seq_len // tc
    route_shape = jax.ShapeDtypeStruct((PEER_HEADS, PEER_NKEYS, T), jnp.uint32)
    route_bf = jax.ShapeDtypeStruct((PEER_HEADS, PEER_NKEYS // 2, T), jnp.uint32)
    route_spec = pl.BlockSpec((PEER_HEADS, PEER_NKEYS, tc), lambda i: (0, 0, i))
    pair_spec = pl.BlockSpec((PEER_HEADS, PEER_NKEYS // 2, tc), lambda i: (0, 0, i))
    return pl.pallas_call(
        _route_kernel,
        grid=(T // tc,),
        in_specs=[
            pl.BlockSpec((tc, D), lambda i: (i, 0)),
            pl.BlockSpec((1, N_MOD, D), lambda i: (i // steps_per_seq, 0, 0)),
            pl.BlockSpec((1, D), lambda i: (0, 0)),
            pl.BlockSpec((HK, D), lambda i: (0, 0)),
            pl.BlockSpec(keys.shape, lambda i: (0, 0, 0, 0)),
        ],
        out_specs=[pl.BlockSpec((D, tc), lambda i: (0, i)), route_spec, route_spec, pair_spec, pair_spec],
        out_shape=[jax.ShapeDtypeStruct((D, T), BF16), route_shape, route_shape, route_bf, route_bf],
        scratch_shapes=[
            pltpu.VMEM((HK, tc), F32),
            pltpu.VMEM((2, PEER_NKEYS, tc), F32),
        ],
        compiler_params=pltpu.CompilerParams(
            dimension_semantics=("arbitrary",), vmem_limit_bytes=VMEM_LIMIT_BYTES),
        name="peer_route",
    )(x2d, mod_l, n2g.reshape(1, D), wqT, keys)


def _gelu_exact(x):
    return 0.5 * x * (1.0 + lax.erf(x * (1.0 / math.sqrt(2.0))))


def _peer_tile(n, lag, n_tiles, n_expert_tiles):
    t = jnp.clip(n - lag, 0, n_tiles - 1)
    return t // n_expert_tiles, t % n_expert_tiles


def _peer_kernel(hT_ref, e1_ref, cnt_ref, r2_ref, e2_ref, u_ref, vT_ref, x_ref, mod_ref, fg_ref, o_ref,
                 acc, p_even, p_odd, *, final_norm, n_tiles, n_expert_tiles):
    te = vT_ref.shape[1]
    tc = hT_ref.shape[1]
    n = pl.program_id(0)
    _, j_val = _peer_tile(n, 1, n_tiles, n_expert_tiles)

    @pl.when(n == 0)
    def _():
        for ref in (p_even, p_odd):
            ref[...] = jnp.zeros(ref.shape, ref.dtype)

    @pl.when(j_val == 0)
    def _():
        acc[...] = jnp.zeros(acc.shape, F32)

    a_per_piece = PEER_PIECE // PEER_NKEYS
    n_pieces = te // PEER_PIECE
    n_chunks = tc // LANES
    groups = PEER_NKEYS // BF16_ROWS
    zero_bf = jnp.zeros((BF16_ROWS, LANES), BF16)

    def step(p_write, p_read):
        def scores(q):
            words_q = slice(q * PEER_PIECE // 2, (q + 1) * PEER_PIECE // 2)
            return jnp.dot(pltpu.bitcast(u_ref[words_q, :], BF16), hT_ref[...],
                           preferred_element_type=F32)

        acc[...] += jnp.dot(pltpu.bitcast(vT_ref[...], BF16), p_read[...], preferred_element_type=F32)
        for q in range(n_pieces):
            pre = scores(q)
            for c in range(n_chunks):
                lanes = slice(c * LANES, (c + 1) * LANES)
                ws = [[None] * groups for _ in range(a_per_piece)]
                for hd in range(PEER_HEADS):
                    for al in range(a_per_piece):
                        cnt_row = _row_as_packed_bf16(cnt_ref, hd, q * a_per_piece + al, lanes)
                        e1_row = _row_as_packed_bf16(e1_ref, hd, q * a_per_piece + al, lanes)
                        for g in range(groups):
                            words = slice(g * SUBLANES, (g + 1) * SUBLANES)
                            r2 = pltpu.bitcast(r2_ref[hd, words, lanes], BF16)
                            e2 = pltpu.bitcast(e2_ref[hd, words, lanes], BF16)
                            term = jnp.where(r2 < cnt_row, e2, zero_bf) * e1_row
                            ws[al][g] = term if hd == 0 else ws[al][g] + term
                for al in range(a_per_piece):
                    for g in range(groups):
                        rl = al * PEER_NKEYS + g * BF16_ROWS
                        act = _gelu_exact(pre[rl:rl + BF16_ROWS, lanes]).astype(BF16)
                        p_write[q * PEER_PIECE + rl:q * PEER_PIECE + rl + BF16_ROWS, lanes] = ws[al][g] * act

    @pl.when(n % 2 == 0)
    def _():
        step(p_even, p_odd)

    @pl.when(n % 2 == 1)
    def _():
        step(p_odd, p_even)

    @pl.when(jnp.logical_and(n >= 1, j_val == n_expert_tiles - 1))
    def _():
        gate2 = mod_ref[0, 5:6, :]
        out = x_ref[...] + gate2 * acc[...].T
        if final_norm:
            out = out * lax.rsqrt(jnp.mean(out * out, axis=-1, keepdims=True) + EPS) * fg_ref[...]
        o_ref[...] = out


def _peer_call(hT, e1, cnt, r2, e2, u_words, vT_words, x2d, mod_l, final_g, seq_len, final_norm):
    T, D = x2d.shape
    E = vT_words.shape[1]
    tc = PEER_COLS
    te = PEER_EXPERT_TILE
    steps_per_seq = seq_len // tc
    n_expert_tiles = E // te
    n_tiles = (T // tc) * n_expert_tiles
    tile = functools.partial(_peer_tile, n_tiles=n_tiles, n_expert_tiles=n_expert_tiles)
    packed_spec = pl.BlockSpec((PEER_HEADS, PEER_NKEYS // 2, tc), lambda n: (0, 0, tile(n, 0)[0]))
    key_spec = pl.BlockSpec((PEER_HEADS, te // PEER_NKEYS, tc), lambda n: (0, tile(n, 0)[1], tile(n, 0)[0]))
    return pl.pallas_call(
        functools.partial(_peer_kernel, final_norm=final_norm, n_tiles=n_tiles, n_expert_tiles=n_expert_tiles),
        grid=(n_tiles + 1,),
        in_specs=[
            pl.BlockSpec((D, tc), lambda n: (0, tile(n, 0)[0])),
            key_spec, key_spec, packed_spec, packed_spec,
            pl.BlockSpec((te // 2, D), lambda n: (tile(n, 0)[1], 0)),
            pl.BlockSpec((D // 2, te), lambda n: (0, tile(n, 1)[1])),
            pl.BlockSpec((tc, D), lambda n: (tile(n, 1)[0], 0)),
            pl.BlockSpec((1, N_MOD, D), lambda n: (tile(n, 1)[0] // steps_per_seq, 0, 0)),
            pl.BlockSpec((1, D), lambda n: (0, 0)),
        ],
        out_specs=pl.BlockSpec((tc, D), lambda n: (tile(n, 1)[0], 0)),
        out_shape=jax.ShapeDtypeStruct((T, D), F32),
        scratch_shapes=[
            pltpu.VMEM((D, tc), F32),
            pltpu.VMEM((te, tc), BF16),
            pltpu.VMEM((te, tc), BF16),
        ],
        compiler_params=pltpu.CompilerParams(
            dimension_semantics=("arbitrary",), vmem_limit_bytes=VMEM_LIMIT_BYTES),
        name="peer_experts",
    )(hT, e1, cnt, r2, e2, u_words, vT_words, x2d, mod_l, final_g.reshape(1, D))


def _pack_kernel(w_ref, o_ref, *, transpose):
    w = w_ref[...]
    if transpose:
        w = w.T
    o_ref[...] = pltpu.bitcast(w.astype(BF16), jnp.uint32)


def _pack_row_pairs(tables, layer, transpose):
    _, R, C = tables.shape
    blk = PACK_ROWS
    if transpose:
        out_shape, out_spec = (C // 2, R), pl.BlockSpec((C // 2, blk), lambda i: (0, i))
    else:
        out_shape, out_spec = (R // 2, C), pl.BlockSpec((blk // 2, C), lambda i: (i, 0))
    return pl.pallas_call(
        functools.partial(_pack_kernel, transpose=transpose),
        grid=(R // blk,),
        in_specs=[pl.BlockSpec((None, blk, C), lambda i: (layer, i, 0))],
        out_specs=out_spec,
        out_shape=jax.ShapeDtypeStruct(out_shape, jnp.uint32),
        compiler_params=pltpu.CompilerParams(
            dimension_semantics=("arbitrary",), vmem_limit_bytes=VMEM_LIMIT_BYTES),
        name="pack_table",
    )(tables)


@jax.jit
def _forward(x, c, ada_w, ada_b, norm1_g, w_in, b_in, conv_w, conv_b, conv_ln_g, conv_ln_b, w_conv_out,
             pool_w, pool_scale, w_out, b_out, norm2_g, peer_wq, peer_keys, peer_u, peer_v, final_g):
    B, S, D = x.shape
    L = ada_w.shape[0]
    c_pad = jnp.zeros((SUBLANES, D), F32).at[:B].set(c)
    mod = _ada_call(c_pad, ada_w, ada_b)[:, :B].reshape(L, B, N_MOD, D)
    for l in range(L):
        x = _mixer_call(x, mod[l], norm1_g[l], w_in[l], b_in[l], conv_w[l], conv_b[l], conv_ln_g[l],
                        conv_ln_b[l], w_conv_out[l], pool_w[l], pool_scale[l], w_out[l], b_out[l])
        x2d = x.reshape(B * S, D)
        hT, e1, cnt, r2, e2 = _route_call(x2d, mod[l], norm2_g[l], peer_wq[l].T.astype(BF16), peer_keys[l], S)
        x2d = _peer_call(hT, e1, cnt, r2, e2, _pack_row_pairs(peer_u, l, transpose=False),
                         _pack_row_pairs(peer_v, l, transpose=True), x2d, mod[l],
                         final_g, S, final_norm=(l == L - 1))
        x = x2d.reshape(B, S, D)
    return x


def kernel(x, c, ada_w, ada_b, norm1_g, w_in, b_in, conv_w, conv_b, conv_ln_g, conv_ln_b, w_conv_out, pool_w,
           pool_scale, w_out, b_out, norm2_g, peer_wq, peer_keys, peer_u, peer_v, final_g):
    return _forward(x, c, ada_w, ada_b, norm1_g, w_in, b_in, conv_w, conv_b, conv_ln_g, conv_ln_b, w_conv_out,
                    pool_w, pool_scale, w_out, b_out, norm2_g, peer_wq, peer_keys, peer_u, peer_v, final_g)
```
